```python
import math, functools
import jax, jax.numpy as jnp
from jax import lax
import numpy as np

D_MODEL = 2048
BATCH = 1
SEQ = 8192
DEPTH = 2
DEC_BATCH = 8
DEC_SEQ = 64
PAST_LEN = 2048

CHUNK = 64
N_A_LAYERS = DEPTH // 2
N_B_LAYERS = DEPTH - N_A_LAYERS

GLA_HEADS = 4
GLA_DK = (D_MODEL // 2) // GLA_HEADS
GLA_DV = D_MODEL // GLA_HEADS
GLA_HK = GLA_HEADS * GLA_DK
GLA_HV = GLA_HEADS * GLA_DV
GLA_LOWRANK = 16
GLA_TAU = 16.0
GLA_IN = 2 * GLA_HK + 2 * GLA_HV + GLA_LOWRANK

FOX_HEADS = 16
FOX_HD = D_MODEL // FOX_HEADS
FOX_FORGET_BIAS = 3.0
Q_BLOCK = 128

D_FF = ((8 * D_MODEL // 3 + 255) // 256) * 256
EPS = 1e-6

kernel_name = 'streaming_gla_fox_yoco'


def rmsnorm(x, g):
    xf = x.astype(jnp.float32)
    y = xf * lax.rsqrt(jnp.mean(xf * xf, axis=-1, keepdims=True) + EPS)
    return (y * g.astype(jnp.float32)).astype(x.dtype)


def swiglu(u, w_in, w_out):
    gate, up = jnp.split(u @ w_in, 2, axis=-1)
    return (jax.nn.silu(gate) * up) @ w_out


def gla_scan(q, k, v, g, s0):
    B, L, H, DK = q.shape
    DV = v.shape[-1]
    C = CHUNK if L % CHUNK == 0 else L
    n = L // C

    def chunks(a):
        return a.astype(jnp.float32).reshape(B, n, C, H, a.shape[-1]).transpose(1, 0, 3, 2, 4)

    causal = jnp.tril(jnp.ones((C, C), dtype=bool))

    def step(S, inp):
        qc, kc, vc, gc = inp
        b = jnp.cumsum(gc, axis=2)
        o_inter = jnp.einsum('bhtd,bhdv->bhtv', qc * jnp.exp(b), S)
        diff = b[:, :, :, None, :] - b[:, :, None, :, :]
        decay = jnp.exp(jnp.where(causal[:, :, None], diff, -jnp.inf))
        att = jnp.einsum('bhtd,bhsd,bhtsd->bhts', qc, kc, decay)
        o_intra = jnp.einsum('bhts,bhsv->bhtv', att, vc)
        b_last = b[:, :, -1:, :]
        S_new = jnp.exp(b_last[:, :, 0, :])[..., None] * S + jnp.einsum(
            'bhsd,bhsv->bhdv', kc * jnp.exp(b_last - b), vc)
        return S_new, o_inter + o_intra

    S, o = lax.scan(step, s0.astype(jnp.float32), (chunks(q), chunks(k), chunks(v), chunks(g)))
    o = o.transpose(1, 0, 3, 2, 4).reshape(B, L, H, DV)
    return o, S


def gla_mixer(u, s0, w_in, w_gk2, b_gk, g_out, w_out):
    B, L, _ = u.shape
    proj = u @ w_in
    q, k, v, gate, gk_low = jnp.split(
        proj, [GLA_HK, 2 * GLA_HK, 2 * GLA_HK + GLA_HV, 2 * GLA_HK + 2 * GLA_HV], axis=-1)
    gk = jax.nn.log_sigmoid((gk_low @ w_gk2 + b_gk).astype(jnp.float32)) / GLA_TAU
    q = q.reshape(B, L, GLA_HEADS, GLA_DK) * (GLA_DK ** -0.5)
    k = k.reshape(B, L, GLA_HEADS, GLA_DK)
    v = v.reshape(B, L, GLA_HEADS, GLA_DV)
    gk = gk.reshape(B, L, GLA_HEADS, GLA_DK)
    o, S = gla_scan(q, k, v, gk, s0)
    gate = gate.reshape(B, L, GLA_HEADS, GLA_DV).astype(jnp.float32)
    o = rmsnorm(o, g_out) * jax.nn.silu(gate)
    return o.reshape(B, L, GLA_HV).astype(u.dtype) @ w_out, S


def shared_kv(h, kv_g, kv_w, kv_b_f, kv_g_k):
    B, L, _ = h.shape
    u = rmsnorm(h, kv_g)
    k, v, f = jnp.split(u @ kv_w, [D_MODEL, 2 * D_MODEL], axis=-1)
    k = rmsnorm(k.reshape(B, L, FOX_HEADS, FOX_HD), kv_g_k)
    v = v.reshape(B, L, FOX_HEADS, FOX_HD)
    logf = jax.nn.log_sigmoid((f + kv_b_f).astype(jnp.float32))
    return k, v, logf


def fox_block(qb, cqb, qi, k, v, ckT, k_idx):
    s = jnp.einsum('bqhd,bkhd->bhqk', qb.astype(jnp.float32), k.astype(jnp.float32)) * (FOX_HD ** -0.5)
    s = s + (cqb.transpose(0, 2, 1)[:, :, :, None] - ckT[:, :, None, :])
    s = jnp.where(k_idx[None, :] <= qi[:, None], s, -jnp.inf)
    p = jax.nn.softmax(s, axis=-1)
    return jnp.einsum('bhqk,bkhd->bqhd', p, v.astype(jnp.float32)).astype(v.dtype)


def fox_mixer(u, k_all, v_all, c_all, w_q, g_q, w_o):
    B, L, _ = u.shape
    Lk = k_all.shape[1]
    q = rmsnorm((u @ w_q).reshape(B, L, FOX_HEADS, FOX_HD), g_q)
    cq = c_all[:, Lk - L:]
    ckT = c_all.transpose(0, 2, 1)
    q_idx = jnp.arange(Lk - L, Lk)
    k_idx = jnp.arange(Lk)
    if L > Q_BLOCK and L % Q_BLOCK == 0:
        nb = L // Q_BLOCK
        qb = q.reshape(B, nb, Q_BLOCK, FOX_HEADS, FOX_HD).swapaxes(0, 1)
        cqb = cq.reshape(B, nb, Q_BLOCK, FOX_HEADS).swapaxes(0, 1)
        qib = q_idx.reshape(nb, Q_BLOCK)
        out = lax.map(lambda a: fox_block(a[0], a[1], a[2], k_all, v_all, ckT, k_idx), (qb, cqb, qib))
        out = out.swapaxes(0, 1).reshape(B, L, FOX_HEADS, FOX_HD)
    else:
        out = fox_block(q, cq, q_idx, k_all, v_all, ckT, k_idx)
    return out.reshape(B, L, D_MODEL) @ w_o


def trunk(x, gla_s0, past_k, past_v, past_logf, g_mix, g_ffn, gla_w_in, gla_w_gk2, gla_b_gk,
          gla_g_out, gla_w_out, kv_g, kv_w, kv_b_f, kv_g_k, fox_w_q, fox_g_q, fox_w_o,
          ffn_w_in, ffn_w_out):
    h = x
    gla_states = []
    k_new = v_new = logf_new = None
    k_all = v_all = c_all = None
    for i in range(DEPTH):
        u = rmsnorm(h, g_mix[i])
        if i < N_A_LAYERS:
            a, s = gla_mixer(u, gla_s0[i], gla_w_in[i], gla_w_gk2[i], gla_b_gk[i], gla_g_out[i], gla_w_out[i])
            gla_states.append(s)
        else:
            if i == N_A_LAYERS:
                k_new, v_new, logf_new = shared_kv(h, kv_g, kv_w, kv_b_f, kv_g_k)
                if past_k is None:
                    k_all, v_all, logf_all = k_new, v_new, logf_new
                else:
                    k_all = jnp.concatenate([past_k.astype(k_new.dtype), k_new], axis=1)
                    v_all = jnp.concatenate([past_v.astype(v_new.dtype), v_new], axis=1)
                    logf_all = jnp.concatenate([past_logf.astype(jnp.float32), logf_new], axis=1)
                c_all = jnp.cumsum(logf_all, axis=1)
            j = i - N_A_LAYERS
            a = fox_mixer(u, k_all, v_all, c_all, fox_w_q[j], fox_g_q[j], fox_w_o[j])
        h = h + a
        h = h + swiglu(rmsnorm(h, g_ffn[i]), ffn_w_in[i], ffn_w_out[i])
    return h, jnp.stack(gla_states).astype(x.dtype), k_new, v_new, logf_new.astype(x.dtype)


def setup_inputs(seed: int = 0) -> dict:
    key = jax.random.key(seed)
    ks = jax.random.split(key, 22)

    def nrm(k, shape, scale):
        return jax.random.normal(k, shape, jnp.float32) * scale

    return {
        'x_prompt': nrm(ks[0], (BATCH, SEQ, D_MODEL), 1.0),
        'x_sample': nrm(ks[1], (DEC_BATCH, DEC_SEQ, D_MODEL), 1.0),
        'state_gla': nrm(ks[2], (N_A_LAYERS, DEC_BATCH, GLA_HEADS, GLA_DK, GLA_DV), 0.1),
        'cache_k': nrm(ks[3], (DEC_BATCH, PAST_LEN, FOX_HEADS, FOX_HD), 1.0),
        'cache_v': nrm(ks[4], (DEC_BATCH, PAST_LEN, FOX_HEADS, FOX_HD), 1.0),
        'cache_logf': jax.nn.log_sigmoid(FOX_FORGET_BIAS + nrm(ks[5], (DEC_BATCH, PAST_LEN, FOX_HEADS), 1.0)),
        'g_mix': 1.0 + nrm(ks[6], (DEPTH, D_MODEL), 0.02),
        'g_ffn': 1.0 + nrm(ks[7], (DEPTH, D_MODEL), 0.02),
        'gla_w_in': nrm(ks[8], (N_A_LAYERS, D_MODEL, GLA_IN), D_MODEL ** -0.5),
        'gla_w_gk2': nrm(ks[9], (N_A_LAYERS, GLA_LOWRANK, GLA_HK), GLA_LOWRANK ** -0.5),
        'gla_b_gk': nrm(ks[10], (N_A_LAYERS, GLA_HK), 0.02),
        'gla_g_out': 1.0 + nrm(ks[11], (N_A_LAYERS, GLA_DV), 0.02),
        'gla_w_out': nrm(ks[12], (N_A_LAYERS, GLA_HV, D_MODEL), GLA_HV ** -0.5),
        'kv_g': 1.0 + nrm(ks[13], (D_MODEL,), 0.02),
        'kv_w': nrm(ks[14], (D_MODEL, 2 * D_MODEL + FOX_HEADS), D_MODEL ** -0.5),
        'kv_b_f': FOX_FORGET_BIAS + nrm(ks[15], (FOX_HEADS,), 0.1),
        'kv_g_k': 1.0 + nrm(ks[16], (FOX_HD,), 0.02),
        'fox_w_q': nrm(ks[17], (N_B_LAYERS, D_MODEL, D_MODEL), D_MODEL ** -0.5),
        'fox_g_q': 1.0 + nrm(ks[18], (N_B_LAYERS, FOX_HD), 0.02),
        'fox_w_o': nrm(ks[19], (N_B_LAYERS, D_MODEL, D_MODEL), D_MODEL ** -0.5),
        'ffn_w_in': nrm(ks[20], (DEPTH, D_MODEL, 2 * D_FF), D_MODEL ** -0.5),
        'ffn_w_out': nrm(ks[21], (DEPTH, D_FF, D_MODEL), D_FF ** -0.5),
    }


def reference(x_prompt, x_sample, state_gla, cache_k, cache_v, cache_logf, g_mix, g_ffn,
              gla_w_in, gla_w_gk2, gla_b_gk, gla_g_out, gla_w_out, kv_g, kv_w, kv_b_f, kv_g_k,
              fox_w_q, fox_g_q, fox_w_o, ffn_w_in, ffn_w_out):
    s0_prompt = jnp.zeros((N_A_LAYERS, x_prompt.shape[0], GLA_HEADS, GLA_DK, GLA_DV), x_prompt.dtype)
    y_prompt, sg_p, k_p, v_p, lf_p = trunk(
        x_prompt, s0_prompt, None, None, None, g_mix, g_ffn, gla_w_in, gla_w_gk2, gla_b_gk,
        gla_g_out, gla_w_out, kv_g, kv_w, kv_b_f, kv_g_k, fox_w_q, fox_g_q, fox_w_o, ffn_w_in, ffn_w_out)
    y_sample, sg_s, k_s, v_s, lf_s = trunk(
        x_sample, state_gla, cache_k, cache_v, cache_logf, g_mix, g_ffn, gla_w_in, gla_w_gk2, gla_b_gk,
        gla_g_out, gla_w_out, kv_g, kv_w, kv_b_f, kv_g_k, fox_w_q, fox_g_q, fox_w_o, ffn_w_in, ffn_w_out)
    return (y_prompt, y_sample, sg_p, k_p, v_p, lf_p, sg_s, k_s, v_s, lf_s)
```

```python
import functools

import jax
import jax.numpy as jnp
import numpy as np
from jax import lax
from jax.experimental import pallas as pl
from jax.experimental.pallas import tpu as pltpu

F32 = jnp.float32
BF16 = jnp.bfloat16

EPS = 1e-6
GLA_HEADS = 4
GLA_TAU = 16.0
GLA_LOWRANK = 16
FOX_HEADS = 16
LANES = 128
ROW_TILE = 512
VMEM_LIMIT = 48 * 1024 * 1024
NEG_BIG = -1e30


def _params(*sem):
    return pltpu.CompilerParams(dimension_semantics=sem, vmem_limit_bytes=VMEM_LIMIT)


def _dot(a, b):
    return jnp.dot(a, b, preferred_element_type=F32)


def _dot_nt(a, b):
    return lax.dot_general(a, b, (((1,), (1,)), ((), ())), preferred_element_type=F32)


def _dot_tn(a, b):
    return lax.dot_general(a, b, (((0,), (0,)), ((), ())), preferred_element_type=F32)


def _split3(x):
    hi = x.astype(BF16)
    r1 = x - hi.astype(F32)
    mid = r1.astype(BF16)
    lo = (r1 - mid.astype(F32)).astype(BF16)
    return hi, mid, lo


def _sum3(dot, pieces, other, other_first):
    hi, mid, lo = pieces
    if other_first:
        return (dot(other, lo) + dot(other, mid)) + dot(other, hi)
    return (dot(lo, other) + dot(mid, other)) + dot(hi, other)


def _rms(x, g):
    return x * lax.rsqrt(jnp.mean(x * x, axis=-1, keepdims=True) + EPS) * g


def _log_sigmoid(x):
    return jnp.minimum(x, 0.0) - jnp.log1p(jnp.exp(-jnp.abs(x)))


def _silu(x):
    return x / (1.0 + jnp.exp(-x))


def _norm_into(u_scr, x_ref, g_ref):
    @pl.when(pl.program_id(1) == 0)
    def _():
        u_scr[...] = _rms(x_ref[...], g_ref[...]).astype(BF16)


def _norm_mm_kernel(x_ref, g_ref, w_ref, o_ref, u_scr):
    _norm_into(u_scr, x_ref, g_ref)
    o_ref[...] = _dot(u_scr[...], w_ref[...]).astype(o_ref.dtype)


def _norm_matmul(x, g, w, *, tn, out_dtype, name):
    m, d = x.shape
    n = w.shape[1]
    return pl.pallas_call(
        _norm_mm_kernel,
        grid=(m // ROW_TILE, n // tn),
        in_specs=[pl.BlockSpec((ROW_TILE, d), lambda i, j: (i, 0)),
                  pl.BlockSpec((1, d), lambda i, j: (0, 0)),
                  pl.BlockSpec((d, tn), lambda i, j: (0, j))],
        out_specs=pl.BlockSpec((ROW_TILE, tn), lambda i, j: (i, j)),
        out_shape=jax.ShapeDtypeStruct((m, n), out_dtype),
        scratch_shapes=[pltpu.VMEM((ROW_TILE, d), BF16)],
        compiler_params=_params("parallel", "arbitrary"),
        name=name,
    )(x, g.reshape(1, d), w)


def _head_proj_kernel(x_ref, g_ref, w_ref, hg_ref, *rest, head_dim, head_norm, scale, emit_f32):
    outs, u_scr = rest[:-1], rest[-1]
    _norm_into(u_scr, x_ref, g_ref)
    acc = _dot(u_scr[...], w_ref[...])
    for h in range(acc.shape[1] // head_dim):
        sl = slice(h * head_dim, (h + 1) * head_dim)
        y = acc[:, sl]
        if head_norm:
            y = _rms(y, hg_ref[...])
        if scale != 1.0:
            y = y * scale
        if emit_f32:
            outs[0][:, sl] = y
        outs[-1][:, sl] = y.astype(BF16)


def _head_proj(x, g, w, head_gain, *, head_dim, head_norm, scale, emit_f32, tn, name):
    m, d = x.shape
    n = w.shape[1]
    out_spec = pl.BlockSpec((ROW_TILE, tn), lambda i, j: (i, j))
    shapes = [jax.ShapeDtypeStruct((m, n), BF16)]
    if emit_f32:
        shapes.insert(0, jax.ShapeDtypeStruct((m, n), F32))
    return pl.pallas_call(
        functools.partial(_head_proj_kernel, head_dim=head_dim, head_norm=head_norm,
                          scale=scale, emit_f32=emit_f32),
        grid=(m // ROW_TILE, n // tn),
        in_specs=[pl.BlockSpec((ROW_TILE, d), lambda i, j: (i, 0)),
                  pl.BlockSpec((1, d), lambda i, j: (0, 0)),
                  pl.BlockSpec((d, tn), lambda i, j: (0, j)),
                  pl.BlockSpec((1, head_dim), lambda i, j: (0, 0))],
        out_specs=[out_spec] * len(shapes),
        out_shape=shapes,
        scratch_shapes=[pltpu.VMEM((ROW_TILE, d), BF16)],
        compiler_params=_params("parallel", "arbitrary"),
        name=name,
    )(x, g.reshape(1, d), w, head_gain.reshape(1, head_dim))


def _ffn_in_kernel(x_ref, g_ref, wg_ref, wu_ref, o_ref, u_scr):
    _norm_into(u_scr, x_ref, g_ref)
    u = u_scr[...]
    gate = _dot(u, wg_ref[...])
    up = _dot(u, wu_ref[...])
    o_ref[...] = (_silu(gate) * up).astype(o_ref.dtype)


def _ffn_in(x, g, w_in, *, tn, name):
    m, d = x.shape
    ff = w_in.shape[1] // 2
    nj = ff // tn
    return pl.pallas_call(
        _ffn_in_kernel,
        grid=(m // ROW_TILE, nj),
        in_specs=[pl.BlockSpec((ROW_TILE, d), lambda i, j: (i, 0)),
                  pl.BlockSpec((1, d), lambda i, j: (0, 0)),
                  pl.BlockSpec((d, tn), lambda i, j: (0, j)),
                  pl.BlockSpec((d, tn), lambda i, j: (0, j + nj))],
        out_specs=pl.BlockSpec((ROW_TILE, tn), lambda i, j: (i, j)),
        out_shape=jax.ShapeDtypeStruct((m, ff), BF16),
        scratch_shapes=[pltpu.VMEM((ROW_TILE, d), BF16)],
        compiler_params=_params("parallel", "arbitrary"),
        name=name,
    )(x, g.reshape(1, d), w_in, w_in)


def _res_mm_kernel(a_ref, w_ref, r_ref, o_ref):
    o_ref[...] = r_ref[...] + _dot(a_ref[...], w_ref[...])


def _res_matmul(a, w, res, *, tn, name):
    m, k = a.shape
    n = w.shape[1]
    return pl.pallas_call(
        _res_mm_kernel,
        grid=(m // ROW_TILE, n // tn),
        in_specs=[pl.BlockSpec((ROW_TILE, k), lambda i, j: (i, 0)),
                  pl.BlockSpec((k, tn), lambda i, j: (0, j)),
                  pl.BlockSpec((ROW_TILE, tn), lambda i, j: (i, j))],
        out_specs=pl.BlockSpec((ROW_TILE, tn), lambda i, j: (i, j)),
        out_shape=jax.ShapeDtypeStruct((m, n), F32),
        compiler_params=_params("parallel", "arbitrary"),
        name=name,
    )(a, w, res)


def _gla_gate_kernel(x_ref, g_ref, wl_ref, w2_ref, b_ref, o_ref):
    u = _rms(x_ref[...], g_ref[...]).astype(BF16)
    low = _dot(u, wl_ref[...]).astype(BF16)
    o_ref[...] = _log_sigmoid(_dot(low, w2_ref[...]) + b_ref[...]) * (1.0 / GLA_TAU)


def _gla_gate(x, g, w_low, w_gk2, b_gk, *, name):
    m, d = x.shape
    n = w_gk2.shape[1]
    return pl.pallas_call(
        _gla_gate_kernel,
        grid=(m // ROW_TILE,),
        in_specs=[pl.BlockSpec((ROW_TILE, d), lambda i: (i, 0)),
                  pl.BlockSpec((1, d), lambda i: (0, 0)),
                  pl.BlockSpec((d, LANES), lambda i: (0, 0)),
                  pl.BlockSpec((LANES, n), lambda i: (0, 0)),
                  pl.BlockSpec((1, n), lambda i: (0, 0))],
        out_specs=pl.BlockSpec((ROW_TILE, n), lambda i: (i, 0)),
        out_shape=jax.ShapeDtypeStruct((m, n), F32),
        compiler_params=_params("parallel"),
        name=name,
    )(x, g.reshape(1, d), w_low, w_gk2, b_gk.reshape(1, n))


def _logf_kernel(x_ref, g_ref, w_ref, wt_ref, b_ref, bt_ref, o_ref, ot_ref):
    u = _rms(x_ref[...], g_ref[...]).astype(BF16)
    o_ref[...] = _log_sigmoid(_dot(u, w_ref[...]) + b_ref[...])
    ot_ref[...] = _log_sigmoid(_dot_nt(wt_ref[...], u) + bt_ref[...])


def _logf_proj(x, g, w_f, b_f, *, name):
    m, d = x.shape
    return pl.pallas_call(
        _logf_kernel,
        grid=(m // ROW_TILE,),
        in_specs=[pl.BlockSpec((ROW_TILE, d), lambda i: (i, 0)),
                  pl.BlockSpec((1, d), lambda i: (0, 0)),
                  pl.BlockSpec((d, LANES), lambda i: (0, 0)),
                  pl.BlockSpec((LANES, d), lambda i: (0, 0)),
                  pl.BlockSpec((1, LANES), lambda i: (0, 0)),
                  pl.BlockSpec((LANES, 1), lambda i: (0, 0))],
        out_specs=[pl.BlockSpec((ROW_TILE, LANES), lambda i: (i, 0)),
                   pl.BlockSpec((LANES, ROW_TILE), lambda i: (0, i))],
        out_shape=[jax.ShapeDtypeStruct((m, LANES), F32),
                   jax.ShapeDtypeStruct((LANES, m), F32)],
        compiler_params=_params("parallel"),
        name=name,
    )(x, g.reshape(1, d), w_f, w_f.T, b_f.reshape(1, LANES), b_f.reshape(LANES, 1))


def _block_ref_rows(b, h):
    c, d = b.shape
    span = 2 * h
    if span >= 8:
        b3 = b.reshape(c // span, span, d)
        return jnp.broadcast_to(b3[:, h:h + 1, :], b3.shape).reshape(c, d)
    b3 = b.reshape(c // 8, 8, d)
    sub = lax.broadcasted_iota(jnp.int32, b3.shape, 1)
    ref = jnp.broadcast_to(b3[:, h:h + 1, :], b3.shape)
    for blk in range(1, 8 // span):
        cand = jnp.broadcast_to(b3[:, blk * span + h:blk * span + h + 1, :], b3.shape)
        ref = jnp.where(sub >= blk * span, cand, ref)
    return ref.reshape(c, d)


def _gla_kernel(q_ref, k_ref, v_ref, gate_ref, gk_ref, s0_ref, gout_ref, o_ref, sn_ref, s_scr, *, chunk):
    c_idx = pl.program_id(2)

    @pl.when(c_idx == 0)
    def _():
        s_scr[...] = s0_ref[0, 0]

    dk = q_ref.shape[1]
    q = q_ref[...].astype(F32) * (dk ** -0.5)
    k = k_ref[...].astype(F32)
    v = v_ref[...]
    row = lax.broadcasted_iota(jnp.int32, (chunk, chunk), 0)
    col = lax.broadcasted_iota(jnp.int32, (chunk, chunk), 1)
    tri = jnp.where(row >= col, 1.0, 0.0).astype(BF16)
    g3 = _split3(gk_ref[...])
    b = _sum3(_dot, g3, tri, other_first=True)
    b_end = b[chunk - 1:chunk, :]
    b_end_col = _sum3(_dot_tn, g3, jnp.ones((chunk, LANES), BF16), other_first=False)

    s_prev = s_scr[...]
    o = _dot((q * jnp.exp(b)).astype(BF16), s_prev.astype(BF16))

    att = jnp.where(row == col, _dot_nt(q.astype(BF16), k.astype(BF16)), 0.0)
    split_bit = jnp.where(row > col, row ^ col, 0)
    sub_row = lax.broadcasted_iota(jnp.int32, (chunk, dk), 0)
    h = 1
    while h < chunk:
        f = jnp.exp(-jnp.abs(b - _block_ref_rows(b, h)))
        z = (jnp.where((sub_row & h) != 0, q, k) * f).astype(BF16)
        att = jnp.where(split_bit >= h, _dot_nt(z, z), att)
        h *= 2
    o = o + _dot(att.astype(BF16), v)

    k_dec = (k * jnp.exp(b_end - b)).astype(BF16)
    decay = jnp.exp(b_end_col)
    decay = jnp.concatenate([decay] * (s_prev.shape[1] // LANES), axis=1)
    s_new = decay * s_prev + _dot_tn(k_dec, v)
    s_scr[...] = s_new

    @pl.when(c_idx == pl.num_programs(2) - 1)
    def _():
        sn_ref[0, 0] = s_new

    o_ref[...] = (_rms(o, gout_ref[...]) * _silu(gate_ref[...].astype(F32))).astype(o_ref.dtype)


def _gla_scan(proj, gk, s0, g_out, *, row0, streams, length, chunk, name):
    heads = GLA_HEADS
    dk = gk.shape[1] // heads
    dv = s0.shape[-1]
    n_chunks = length // chunk
    base = row0 // chunk
    kq, vq = dk * heads // dk, 2 * dk * heads // dv

    def rows(s, h, c):
        return base + s * n_chunks + c

    def rows_out(s, h, c):
        return s * n_chunks + c

    return pl.pallas_call(
        functools.partial(_gla_kernel, chunk=chunk),
        grid=(streams, heads, n_chunks),
        in_specs=[pl.BlockSpec((chunk, dk), lambda s, h, c: (rows(s, h, c), h)),
                  pl.BlockSpec((chunk, dk), lambda s, h, c: (rows(s, h, c), kq + h)),
                  pl.BlockSpec((chunk, dv), lambda s, h, c: (rows(s, h, c), vq + h)),
                  pl.BlockSpec((chunk, dv), lambda s, h, c: (rows(s, h, c), vq + heads + h)),
                  pl.BlockSpec((chunk, dk), lambda s, h, c: (rows(s, h, c), h)),
                  pl.BlockSpec((1, 1, dk, dv), lambda s, h, c: (s, h, 0, 0)),
                  pl.BlockSpec((1, dv), lambda s, h, c: (0, 0))],
        out_specs=[pl.BlockSpec((chunk, dv), lambda s, h, c: (rows_out(s, h, c), h)),
                   pl.BlockSpec((1, 1, dk, dv), lambda s, h, c: (s, h, 0, 0))],
        out_shape=[jax.ShapeDtypeStruct((streams * length, heads * dv), BF16),
                   jax.ShapeDtypeStruct((streams, heads, dk, dv), F32)],
        scratch_shapes=[pltpu.VMEM((dk, dv), F32)],
        compiler_params=_params("parallel", "parallel", "arbitrary"),
        name=name,
    )(proj, proj, proj, proj, gk, s0, g_out.reshape(1, dv))


def _cumsum_kernel(x_ref, xt_ref, c_ref, ct_ref, row_scr, col_scr, *, tile):
    @pl.when(pl.program_id(0) == 0)
    def _():
        row_scr[...] = jnp.zeros_like(row_scr)
        col_scr[...] = jnp.zeros_like(col_scr)

    r = lax.broadcasted_iota(jnp.int32, (tile, tile), 0)
    c = lax.broadcasted_iota(jnp.int32, (tile, tile), 1)
    x = x_ref[...]
    xt = xt_ref[...]
    lower = jnp.where(r >= c, 1.0, 0.0).astype(BF16)
    upper = jnp.where(r <= c, 1.0, 0.0).astype(BF16)
    c_ref[...] = row_scr[...] + _sum3(_dot, _split3(x), lower, other_first=True)
    ct_ref[...] = col_scr[...] + _sum3(_dot, _split3(xt), upper, other_first=False)
    row_scr[...] += jnp.sum(x, axis=0, keepdims=True)
    col_scr[...] += jnp.sum(xt, axis=1, keepdims=True)


def _cumsum_tokens(x, xt, *, length, tile, name):
    return pl.pallas_call(
        functools.partial(_cumsum_kernel, tile=tile),
        grid=(length // tile,),
        in_specs=[pl.BlockSpec((tile, LANES), lambda i: (i, 0)),
                  pl.BlockSpec((LANES, tile), lambda i: (0, i))],
        out_specs=[pl.BlockSpec((tile, LANES), lambda i: (i, 0)),
                   pl.BlockSpec((LANES, tile), lambda i: (0, i))],
        out_shape=[jax.ShapeDtypeStruct((length, LANES), F32),
                   jax.ShapeDtypeStruct((LANES, length), F32)],
        scratch_shapes=[pltpu.VMEM((1, LANES), F32), pltpu.VMEM((LANES, 1), F32)],
        compiler_params=_params("arbitrary"),
        name=name,
    )(x, xt)


def _cumsum_cached_kernel(past_ref, pastt_ref, new_ref, newt_ref, cq_ref, cpt_ref, cnt_ref, *, tile):
    past_t = pastt_ref[0]
    past_len = past_t.shape[1]
    r = lax.broadcasted_iota(jnp.int32, (tile, tile), 0)
    c = lax.broadcasted_iota(jnp.int32, (tile, tile), 1)
    upper = jnp.where(r <= c, 1.0, 0.0).astype(BF16)
    carry = jnp.zeros((past_t.shape[0], 1), F32)
    for t in range(past_len // tile):
        xt = past_t[:, t * tile:(t + 1) * tile]
        cpt_ref[0, :, t * tile:(t + 1) * tile] = carry + _sum3(_dot, _split3(xt), upper, other_first=False)
        carry = carry + jnp.sum(xt, axis=1, keepdims=True)
    n = newt_ref.shape[2]
    rn = lax.broadcasted_iota(jnp.int32, (n, n), 0)
    cn = lax.broadcasted_iota(jnp.int32, (n, n), 1)
    cnt_ref[0] = carry + _sum3(_dot, _split3(newt_ref[0]),
                               jnp.where(rn <= cn, 1.0, 0.0).astype(BF16), other_first=False)
    heads = past_t.shape[0]
    past_total = jnp.sum(past_ref[0], axis=0, keepdims=True)
    c_new = _sum3(_dot, _split3(new_ref[...]), jnp.where(rn >= cn, 1.0, 0.0).astype(BF16), other_first=True)
    cq_ref[0] = past_total + c_new[:, :heads]


def _cumsum_cached(past, past_t, new, new_t, *, row0, name):
    streams, past_len, heads = past.shape
    n = new_t.shape[2]
    base = row0 // n
    return pl.pallas_call(
        functools.partial(_cumsum_cached_kernel, tile=512),
        grid=(streams,),
        in_specs=[pl.BlockSpec((1, past_len, heads), lambda s: (s, 0, 0)),
                  pl.BlockSpec((1, heads, past_len), lambda s: (s, 0, 0)),
                  pl.BlockSpec((n, LANES), lambda s: (base + s, 0)),
                  pl.BlockSpec((1, heads, n), lambda s: (s, 0, 0))],
        out_specs=[pl.BlockSpec((1, n, heads), lambda s: (s, 0, 0)),
                   pl.BlockSpec((1, heads, past_len), lambda s: (s, 0, 0)),
                   pl.BlockSpec((1, heads, n), lambda s: (s, 0, 0))],
        out_shape=[jax.ShapeDtypeStruct((streams, n, heads), F32),
                   jax.ShapeDtypeStruct((streams, heads, past_len), F32),
                   jax.ShapeDtypeStruct((streams, heads, n), F32)],
        compiler_params=_params("parallel"),
        name=name,
    )(past, past_t, new, new_t)


def _fox_prompt_kernel(qi_ref, kj_ref, q_ref, k_ref, v_ref, cq_ref, ck_ref, o_ref, m_scr, l_scr, acc_scr):
    head = pl.program_id(0)
    step = pl.program_id(1)
    qi = qi_ref[step]
    kj = kj_ref[step]
    tq, tk = q_ref.shape[0], k_ref.shape[0]

    @pl.when(kj == 0)
    def _():
        m_scr[...] = jnp.full_like(m_scr, NEG_BIG)
        l_scr[...] = jnp.zeros_like(l_scr)
        acc_scr[...] = jnp.zeros_like(acc_scr)

    def update(masked):
        lane = lax.broadcasted_iota(jnp.int32, cq_ref.shape, 1)
        cq = jnp.sum(jnp.where(lane == head, cq_ref[...], 0.0), axis=1, keepdims=True)
        s = _dot_nt(q_ref[...], k_ref[...]) + (cq - ck_ref[0])
        if masked:
            row = lax.broadcasted_iota(jnp.int32, (tq, tk), 0)
            col = lax.broadcasted_iota(jnp.int32, (tq, tk), 1)
            s = jnp.where(col <= row, s, NEG_BIG)
        m_prev = m_scr[...]
        m_new = jnp.maximum(m_prev, jnp.max(s, axis=1, keepdims=True))
        alpha = jnp.exp(m_prev - m_new)
        p = jnp.exp(s - m_new)
        l_scr[...] = alpha * l_scr[...] + jnp.sum(p, axis=1, keepdims=True)
        acc_scr[...] = alpha * acc_scr[...] + _dot(p.astype(BF16), v_ref[...])
        m_scr[...] = m_new

    @pl.when(kj < qi)
    def _():
        update(False)

    @pl.when(kj == qi)
    def _():
        update(True)
        o_ref[...] = (acc_scr[...] / l_scr[...]).astype(o_ref.dtype)


def _fox_prompt(q, k, v, c, c_t, *, length, tile, name):
    hd = LANES
    heads = q.shape[1] // hd
    nq = length // tile
    pairs = [(i, j) for i in range(nq) for j in range(i + 1)]
    qi = jnp.asarray(np.array([p[0] for p in pairs], np.int32))
    kj = jnp.asarray(np.array([p[1] for p in pairs], np.int32))
    grid_spec = pltpu.PrefetchScalarGridSpec(
        num_scalar_prefetch=2,
        grid=(heads, len(pairs)),
        in_specs=[pl.BlockSpec((tile, hd), lambda h, s, qi, kj: (qi[s], h)),
                  pl.BlockSpec((tile, hd), lambda h, s, qi, kj: (kj[s], h)),
                  pl.BlockSpec((tile, hd), lambda h, s, qi, kj: (kj[s], h)),
                  pl.BlockSpec((tile, LANES), lambda h, s, qi, kj: (qi[s], 0)),
                  pl.BlockSpec((1, 1, tile), lambda h, s, qi, kj: (h, 0, kj[s]))],
        out_specs=pl.BlockSpec((tile, hd), lambda h, s, qi, kj: (qi[s], h)),
        scratch_shapes=[pltpu.VMEM((tile, 1), F32), pltpu.VMEM((tile, 1), F32), pltpu.VMEM((tile, hd), F32)],
    )
    return pl.pallas_call(
        _fox_prompt_kernel,
        grid_spec=grid_spec,
        out_shape=jax.ShapeDtypeStruct((length, heads * hd), BF16),
        compiler_params=_params("parallel", "arbitrary"),
        name=name,
    )(qi, kj, q, k, v, c, c_t.reshape(c_t.shape[0], 1, c_t.shape[1]))


def _fox_cached_kernel(q_ref, kp_ref, vp_ref, kn_ref, vn_ref, cq_ref, cp_ref, cn_ref, o_ref):
    head = pl.program_id(1)
    q = q_ref[...]
    n = q.shape[0]
    lane = lax.broadcasted_iota(jnp.int32, cq_ref.shape[1:], 1)
    cq = jnp.sum(jnp.where(lane == head, cq_ref[0], 0.0), axis=1, keepdims=True)
    s_past = _dot_nt(q, kp_ref[0].astype(BF16)) + (cq - cp_ref[0, 0])
    s_new = _dot_nt(q, kn_ref[...]) + (cq - cn_ref[0, 0])
    row = lax.broadcasted_iota(jnp.int32, (n, n), 0)
    col = lax.broadcasted_iota(jnp.int32, (n, n), 1)
    s_new = jnp.where(col <= row, s_new, NEG_BIG)
    m = jnp.maximum(jnp.max(s_past, axis=1, keepdims=True), jnp.max(s_new, axis=1, keepdims=True))
    p_past = jnp.exp(s_past - m)
    p_new = jnp.exp(s_new - m)
    denom = jnp.sum(p_past, axis=1, keepdims=True) + jnp.sum(p_new, axis=1, keepdims=True)
    acc = _dot(p_past.astype(BF16), vp_ref[0].astype(BF16)) + _dot(p_new.astype(BF16), vn_ref[...])
    o_ref[...] = (acc / denom).astype(o_ref.dtype)


def _fox_cached(q, k_new, v_new, cache_k, cache_v, cq, cp_t, cn_t, *, row0, name):
    hd = LANES
    streams, n, heads = cq.shape
    past_len = cache_k.shape[1]
    base = row0 // n
    return pl.pallas_call(
        _fox_cached_kernel,
        grid=(streams, heads),
        in_specs=[pl.BlockSpec((n, hd), lambda s, h: (base + s, h)),
                  pl.BlockSpec((1, past_len, hd), lambda s, h: (s, 0, h)),
                  pl.BlockSpec((1, past_len, hd), lambda s, h: (s, 0, h)),
                  pl.BlockSpec((n, hd), lambda s, h: (base + s, h)),
                  pl.BlockSpec((n, hd), lambda s, h: (base + s, h)),
                  pl.BlockSpec((1, n, heads), lambda s, h: (s, 0, 0)),
                  pl.BlockSpec((1, 1, 1, past_len), lambda s, h: (s, h, 0, 0)),
                  pl.BlockSpec((1, 1, 1, n), lambda s, h: (s, h, 0, 0))],
        out_specs=pl.BlockSpec((n, hd), lambda s, h: (s, h)),
        out_shape=jax.ShapeDtypeStruct((streams * n, heads * hd), BF16),
        compiler_params=_params("parallel", "parallel"),
        name=name,
    )(q, cache_k, cache_v, k_new, v_new, cq,
      cp_t.reshape(streams, heads, 1, past_len), cn_t.reshape(streams, heads, 1, n))


def kernel(x_prompt, x_sample, state_gla, cache_k, cache_v, cache_logf, g_mix, g_ffn, gla_w_in, gla_w_gk2, gla_b_gk, gla_g_out, gla_w_out, kv_g, kv_w, kv_b_f, kv_g_k, fox_w_q, fox_g_q, fox_w_o, ffn_w_in, ffn_w_out):
    batch, seq, d = x_prompt.shape
    dec_batch, dec_seq, _ = x_sample.shape
    assert batch == 1 and g_mix.shape[0] == 2 and state_gla.shape[0] == 1
    heads = FOX_HEADS
    hd = d // heads
    n_p = batch * seq
    gla_hk = gla_w_gk2.shape[2]
    gla_hv = gla_w_out.shape[1]
    gla_main = 2 * gla_hk + 2 * gla_hv
    gla_dk, gla_dv = gla_hk // GLA_HEADS, gla_hv // GLA_HEADS
    past_len = cache_k.shape[1]

    x = jnp.concatenate([x_prompt.reshape(n_p, d), x_sample.reshape(dec_batch * dec_seq, d)], axis=0)

    w_gla = gla_w_in[0]
    proj = _norm_matmul(x, g_mix[0], w_gla[:, :gla_main].astype(BF16), tn=1024, out_dtype=BF16, name="gla_proj")
    w_low = jnp.pad(w_gla[:, gla_main:], ((0, 0), (0, LANES - GLA_LOWRANK))).astype(BF16)
    w_gk2 = jnp.pad(gla_w_gk2[0], ((0, LANES - GLA_LOWRANK), (0, 0))).astype(BF16)
    gk = _gla_gate(x, g_mix[0], w_low, w_gk2, gla_b_gk[0], name="gla_gate")
    s0_prompt = jnp.zeros((batch, GLA_HEADS, gla_dk, gla_dv), F32)
    og_p, state_p = _gla_scan(proj, gk, s0_prompt, gla_g_out[0], row0=0, streams=batch, length=seq,
                              chunk=256, name="gla_scan_prompt")
    og_s, state_s = _gla_scan(proj, gk, state_gla[0], gla_g_out[0], row0=n_p, streams=dec_batch,
                              length=dec_seq, chunk=dec_seq, name="gla_scan_sample")
    og = jnp.concatenate([og_p, og_s], axis=0)
    h = _res_matmul(og, gla_w_out[0].astype(BF16), x, tn=1024, name="gla_out")

    act = _ffn_in(h, g_ffn[0], ffn_w_in[0].astype(BF16), tn=512, name="ffn0_in")
    h = _res_matmul(act, ffn_w_out[0].astype(BF16), h, tn=512, name="ffn0_out")

    k32, k16 = _head_proj(h, kv_g, kv_w[:, :d].astype(BF16), kv_g_k, head_dim=hd, head_norm=True,
                          scale=1.0, emit_f32=True, tn=1024, name="kv_k")
    v32, v16 = _head_proj(h, kv_g, kv_w[:, d:2 * d].astype(BF16), kv_g_k, head_dim=hd, head_norm=False,
                          scale=1.0, emit_f32=True, tn=1024, name="kv_v")
    (q16,) = _head_proj(h, g_mix[1], fox_w_q[0].astype(BF16), fox_g_q[0], head_dim=hd, head_norm=True,
                        scale=hd ** -0.5, emit_f32=False, tn=1024, name="fox_q")
    w_f = jnp.pad(kv_w[:, 2 * d:], ((0, 0), (0, LANES - heads))).astype(BF16)
    b_f = jnp.pad(kv_b_f, (0, LANES - heads))
    logf, logf_t = _logf_proj(h, kv_g, w_f, b_f, name="kv_logf")

    c_p, c_p_t = _cumsum_tokens(logf, logf_t, length=n_p, tile=512, name="logf_cumsum_prompt")
    attn_p = _fox_prompt(q16, k16, v16, c_p, c_p_t, length=n_p, tile=1024, name="fox_attn_prompt")
    new_t = logf_t[:heads, n_p:].reshape(heads, dec_batch, dec_seq).transpose(1, 0, 2)
    cq_s, cp_t, cn_t = _cumsum_cached(cache_logf, cache_logf.transpose(0, 2, 1), logf, new_t,
                                      row0=n_p, name="logf_cumsum_sample")
    attn_s = _fox_cached(q16, k16, v16, cache_k.reshape(dec_batch, past_len, d),
                         cache_v.reshape(dec_batch, past_len, d), cq_s, cp_t, cn_t,
                         row0=n_p, name="fox_attn_sample")
    attn = jnp.concatenate([attn_p, attn_s], axis=0)
    h = _res_matmul(attn, fox_w_o[0].astype(BF16), h, tn=1024, name="fox_out")

    act = _ffn_in(h, g_ffn[1], ffn_w_in[1].astype(BF16), tn=512, name="ffn1_in")
    y = _res_matmul(act, ffn_w_out[1].astype(BF16), h, tn=512, name="ffn1_out")

    def split(a, tail):
        return a[:n_p].reshape((batch, seq) + tail), a[n_p:].reshape((dec_batch, dec_seq) + tail)

    y_p, y_s = split(y, (d,))
    k_p, k_s = split(k32, (heads, hd))
    v_p, v_s = split(v32, (heads, hd))
    lf_p, lf_s = split(logf[:, :heads], (heads,))
    return (y_p, y_s, state_p[None], k_p, v_p, lf_p, state_s[None], k_s, v_s, lf_s)
```

```python
import functools
import math

import jax
import jax.numpy as jnp
import numpy as np
from jax import lax
from jax.experimental import pallas as pl
from jax.experimental.pallas import tpu as pltpu

F32 = jnp.float32
BF16 = jnp.bfloat16

EPS = 1e-6
GLA_HEADS = 4
GLA_TAU = 16.0
GLA_LOWRANK = 16
FOX_HEADS = 16
LANES = 128
ROW_TILE = 512
VMEM_LIMIT = 48 * 1024 * 1024
NEG_BIG = -1e30
QUERY_CHUNK = 256
EXP_ROWS = 128
SUM_ROWS = 16
LOG2E = math.log2(math.e)


def _params(*sem):
    return pltpu.CompilerParams(dimension_semantics=sem, vmem_limit_bytes=VMEM_LIMIT)


def _dot(a, b):
    return jnp.dot(a, b, preferred_element_type=F32)


def _dot_nt(a, b):
    return lax.dot_general(a, b, (((1,), (1,)), ((), ())), preferred_element_type=F32)


def _dot_tn(a, b):
    return lax.dot_general(a, b, (((0,), (0,)), ((), ())), preferred_element_type=F32)


def _split3(x):
    hi = x.astype(BF16)
    r1 = x - hi.astype(F32)
    mid = r1.astype(BF16)
    lo = (r1 - mid.astype(F32)).astype(BF16)
    return hi, mid, lo


def _sum3(dot, pieces, other, other_first):
    hi, mid, lo = pieces
    if other_first:
        return (dot(other, lo) + dot(other, mid)) + dot(other, hi)
    return (dot(lo, other) + dot(mid, other)) + dot(hi, other)


def _rms(x, g):
    return x * lax.rsqrt(jnp.mean(x * x, axis=-1, keepdims=True) + EPS) * g


def _log_sigmoid(x):
    return jnp.minimum(x, 0.0) - jnp.log1p(jnp.exp(-jnp.abs(x)))


def _silu(x):
    return x / (1.0 + jnp.exp(-x))


def _norm_into(u_scr, x_ref, g_ref):
    @pl.when(pl.program_id(1) == 0)
    def _():
        u_scr[...] = _rms(x_ref[...], g_ref[...]).astype(BF16)


def _norm_mm_kernel(x_ref, g_ref, w_ref, o_ref, u_scr):
    _norm_into(u_scr, x_ref, g_ref)
    o_ref[...] = _dot(u_scr[...], w_ref[...]).astype(o_ref.dtype)


def _norm_matmul(x, g, w, *, tn, out_dtype, name):
    m, d = x.shape
    n = w.shape[1]
    return pl.pallas_call(
        _norm_mm_kernel,
        grid=(m // ROW_TILE, n // tn),
        in_specs=[pl.BlockSpec((ROW_TILE, d), lambda i, j: (i, 0)),
                  pl.BlockSpec((1, d), lambda i, j: (0, 0)),
                  pl.BlockSpec((d, tn), lambda i, j: (0, j))],
        out_specs=pl.BlockSpec((ROW_TILE, tn), lambda i, j: (i, j)),
        out_shape=jax.ShapeDtypeStruct((m, n), out_dtype),
        scratch_shapes=[pltpu.VMEM((ROW_TILE, d), BF16)],
        compiler_params=_params("parallel", "arbitrary"),
        name=name,
    )(x, g.reshape(1, d), w)


def _head_proj_kernel(x_ref, g_ref, w_ref, hg_ref, *rest, head_dim, head_norm, scale, emit_f32, emit_t):
    outs, u_scr = rest[:-1], rest[-1]
    _norm_into(u_scr, x_ref, g_ref)
    acc = _dot(u_scr[...], w_ref[...])
    for h in range(acc.shape[1] // head_dim):
        sl = slice(h * head_dim, (h + 1) * head_dim)
        y = acc[:, sl]
        if head_norm:
            y = _rms(y, hg_ref[...])
        if scale != 1.0:
            y = y * scale
        if emit_f32:
            outs[0][:, sl] = y
        outs[int(emit_f32)][:, sl] = y.astype(BF16)
        if emit_t:
            outs[-1][h] = y.T.astype(BF16)


def _head_proj(x, g, w, head_gain, *, head_dim, head_norm, scale, emit_f32, emit_t, tn, name):
    m, d = x.shape
    n = w.shape[1]
    out_spec = pl.BlockSpec((ROW_TILE, tn), lambda i, j: (i, j))
    specs, shapes = [out_spec], [jax.ShapeDtypeStruct((m, n), BF16)]
    if emit_f32:
        specs.insert(0, out_spec)
        shapes.insert(0, jax.ShapeDtypeStruct((m, n), F32))
    if emit_t:
        specs.append(pl.BlockSpec((tn // head_dim, head_dim, ROW_TILE), lambda i, j: (j, 0, i)))
        shapes.append(jax.ShapeDtypeStruct((n // head_dim, head_dim, m), BF16))
    return pl.pallas_call(
        functools.partial(_head_proj_kernel, head_dim=head_dim, head_norm=head_norm,
                          scale=scale, emit_f32=emit_f32, emit_t=emit_t),
        grid=(m // ROW_TILE, n // tn),
        in_specs=[pl.BlockSpec((ROW_TILE, d), lambda i, j: (i, 0)),
                  pl.BlockSpec((1, d), lambda i, j: (0, 0)),
                  pl.BlockSpec((d, tn), lambda i, j: (0, j)),
                  pl.BlockSpec((1, head_dim), lambda i, j: (0, 0))],
        out_specs=specs,
        out_shape=shapes,
        scratch_shapes=[pltpu.VMEM((ROW_TILE, d), BF16)],
        compiler_params=_params("parallel", "arbitrary"),
        name=name,
    )(x, g.reshape(1, d), w, head_gain.reshape(1, head_dim))


def _ffn_in_kernel(x_ref, g_ref, wg_ref, wu_ref, o_ref, u_scr):
    _norm_into(u_scr, x_ref, g_ref)
    u = u_scr[...]
    gate = _dot(u, wg_ref[...])
    up = _dot(u, wu_ref[...])
    o_ref[...] = (_silu(gate) * up).astype(o_ref.dtype)


def _ffn_in(x, g, w_in, *, tn, name):
    m, d = x.shape
    ff = w_in.shape[1] // 2
    nj = ff // tn
    return pl.pallas_call(
        _ffn_in_kernel,
        grid=(m // ROW_TILE, nj),
        in_specs=[pl.BlockSpec((ROW_TILE, d), lambda i, j: (i, 0)),
                  pl.BlockSpec((1, d), lambda i, j: (0, 0)),
                  pl.BlockSpec((d, tn), lambda i, j: (0, j)),
                  pl.BlockSpec((d, tn), lambda i, j: (0, j + nj))],
        out_specs=pl.BlockSpec((ROW_TILE, tn), lambda i, j: (i, j)),
        out_shape=jax.ShapeDtypeStruct((m, ff), BF16),
        scratch_shapes=[pltpu.VMEM((ROW_TILE, d), BF16)],
        compiler_params=_params("parallel", "arbitrary"),
        name=name,
    )(x, g.reshape(1, d), w_in, w_in)


def _res_mm_kernel(a_ref, w_ref, r_ref, o_ref):
    o_ref[...] = r_ref[...] + _dot(a_ref[...], w_ref[...])


def _res_matmul(a, w, res, *, tn, name):
    m, k = a.shape
    n = w.shape[1]
    return pl.pallas_call(
        _res_mm_kernel,
        grid=(m // ROW_TILE, n // tn),
        in_specs=[pl.BlockSpec((ROW_TILE, k), lambda i, j: (i, 0)),
                  pl.BlockSpec((k, tn), lambda i, j: (0, j)),
                  pl.BlockSpec((ROW_TILE, tn), lambda i, j: (i, j))],
        out_specs=pl.BlockSpec((ROW_TILE, tn), lambda i, j: (i, j)),
        out_shape=jax.ShapeDtypeStruct((m, n), F32),
        compiler_params=_params("parallel", "arbitrary"),
        name=name,
    )(a, w, res)


def _gla_gate_kernel(x_ref, g_ref, wl_ref, w2_ref, b_ref, o_ref):
    u = _rms(x_ref[...], g_ref[...]).astype(BF16)
    low = _dot(u, wl_ref[...]).astype(BF16)
    o_ref[...] = _log_sigmoid(_dot(low, w2_ref[...]) + b_ref[...]) * (1.0 / GLA_TAU)


def _gla_gate(x, g, w_low, w_gk2, b_gk, *, name):
    m, d = x.shape
    n = w_gk2.shape[1]
    return pl.pallas_call(
        _gla_gate_kernel,
        grid=(m // ROW_TILE,),
        in_specs=[pl.BlockSpec((ROW_TILE, d), lambda i: (i, 0)),
                  pl.BlockSpec((1, d), lambda i: (0, 0)),
                  pl.BlockSpec((d, LANES), lambda i: (0, 0)),
                  pl.BlockSpec((LANES, n), lambda i: (0, 0)),
                  pl.BlockSpec((1, n), lambda i: (0, 0))],
        out_specs=pl.BlockSpec((ROW_TILE, n), lambda i: (i, 0)),
        out_shape=jax.ShapeDtypeStruct((m, n), F32),
        compiler_params=_params("parallel"),
        name=name,
    )(x, g.reshape(1, d), w_low, w_gk2, b_gk.reshape(1, n))


def _logf_kernel(x_ref, g_ref, w_ref, wt_ref, b_ref, bt_ref, o_ref, ot_ref):
    u = _rms(x_ref[...], g_ref[...]).astype(BF16)
    o_ref[...] = _log_sigmoid(_dot(u, w_ref[...]) + b_ref[...])
    ot_ref[...] = _log_sigmoid(_dot_nt(wt_ref[...], u) + bt_ref[...])


def _logf_proj(x, g, w_f, b_f, *, name):
    m, d = x.shape
    return pl.pallas_call(
        _logf_kernel,
        grid=(m // ROW_TILE,),
        in_specs=[pl.BlockSpec((ROW_TILE, d), lambda i: (i, 0)),
                  pl.BlockSpec((1, d), lambda i: (0, 0)),
                  pl.BlockSpec((d, LANES), lambda i: (0, 0)),
                  pl.BlockSpec((LANES, d), lambda i: (0, 0)),
                  pl.BlockSpec((1, LANES), lambda i: (0, 0)),
                  pl.BlockSpec((LANES, 1), lambda i: (0, 0))],
        out_specs=[pl.BlockSpec((ROW_TILE, LANES), lambda i: (i, 0)),
                   pl.BlockSpec((LANES, ROW_TILE), lambda i: (0, i))],
        out_shape=[jax.ShapeDtypeStruct((m, LANES), F32),
                   jax.ShapeDtypeStruct((LANES, m), F32)],
        compiler_params=_params("parallel"),
        name=name,
    )(x, g.reshape(1, d), w_f, w_f.T, b_f.reshape(1, LANES), b_f.reshape(LANES, 1))


def _block_ref_rows(b, h):
    c, d = b.shape
    span = 2 * h
    if span >= 8:
        b3 = b.reshape(c // span, span, d)
        return jnp.broadcast_to(b3[:, h:h + 1, :], b3.shape).reshape(c, d)
    b3 = b.reshape(c // 8, 8, d)
    sub = lax.broadcasted_iota(jnp.int32, b3.shape, 1)
    ref = jnp.broadcast_to(b3[:, h:h + 1, :], b3.shape)
    for blk in range(1, 8 // span):
        cand = jnp.broadcast_to(b3[:, blk * span + h:blk * span + h + 1, :], b3.shape)
        ref = jnp.where(sub >= blk * span, cand, ref)
    return ref.reshape(c, d)


def _gla_kernel(q_ref, k_ref, v_ref, gate_ref, gk_ref, s0_ref, gout_ref, o_ref, sn_ref, s_scr, *, chunk):
    c_idx = pl.program_id(2)

    @pl.when(c_idx == 0)
    def _():
        s_scr[...] = s0_ref[0, 0]

    dk = q_ref.shape[1]
    q = q_ref[...].astype(F32) * (dk ** -0.5)
    k = k_ref[...].astype(F32)
    v = v_ref[...]
    row = lax.broadcasted_iota(jnp.int32, (chunk, chunk), 0)
    col = lax.broadcasted_iota(jnp.int32, (chunk, chunk), 1)
    tri = jnp.where(row >= col, 1.0, 0.0).astype(BF16)
    g3 = _split3(gk_ref[...])
    b = _sum3(_dot, g3, tri, other_first=True)
    b_end = b[chunk - 1:chunk, :]
    b_end_col = _sum3(_dot_tn, g3, jnp.ones((chunk, LANES), BF16), other_first=False)

    s_prev = s_scr[...]
    o = _dot((q * jnp.exp(b)).astype(BF16), s_prev.astype(BF16))

    att = jnp.where(row == col, _dot_nt(q.astype(BF16), k.astype(BF16)), 0.0)
    split_bit = jnp.where(row > col, row ^ col, 0)
    sub_row = lax.broadcasted_iota(jnp.int32, (chunk, dk), 0)
    h = 1
    while h < chunk:
        f = jnp.exp(-jnp.abs(b - _block_ref_rows(b, h)))
        z = (jnp.where((sub_row & h) != 0, q, k) * f).astype(BF16)
        att = jnp.where(split_bit >= h, _dot_nt(z, z), att)
        h *= 2
    o = o + _dot(att.astype(BF16), v)

    k_dec = (k * jnp.exp(b_end - b)).astype(BF16)
    decay = jnp.exp(b_end_col)
    decay = jnp.concatenate([decay] * (s_prev.shape[1] // LANES), axis=1)
    s_new = decay * s_prev + _dot_tn(k_dec, v)
    s_scr[...] = s_new

    @pl.when(c_idx == pl.num_programs(2) - 1)
    def _():
        sn_ref[0, 0] = s_new

    o_ref[...] = (_rms(o, gout_ref[...]) * _silu(gate_ref[...].astype(F32))).astype(o_ref.dtype)


def _gla_scan(proj, gk, s0, g_out, *, row0, streams, length, chunk, name):
    heads = GLA_HEADS
    dk = gk.shape[1] // heads
    dv = s0.shape[-1]
    n_chunks = length // chunk
    base = row0 // chunk
    kq, vq = dk * heads // dk, 2 * dk * heads // dv

    def rows(s, h, c):
        return base + s * n_chunks + c

    def rows_out(s, h, c):
        return s * n_chunks + c

    return pl.pallas_call(
        functools.partial(_gla_kernel, chunk=chunk),
        grid=(streams, heads, n_chunks),
        in_specs=[pl.BlockSpec((chunk, dk), lambda s, h, c: (rows(s, h, c), h)),
                  pl.BlockSpec((chunk, dk), lambda s, h, c: (rows(s, h, c), kq + h)),
                  pl.BlockSpec((chunk, dv), lambda s, h, c: (rows(s, h, c), vq + h)),
                  pl.BlockSpec((chunk, dv), lambda s, h, c: (rows(s, h, c), vq + heads + h)),
                  pl.BlockSpec((chunk, dk), lambda s, h, c: (rows(s, h, c), h)),
                  pl.BlockSpec((1, 1, dk, dv), lambda s, h, c: (s, h, 0, 0)),
                  pl.BlockSpec((1, dv), lambda s, h, c: (0, 0))],
        out_specs=[pl.BlockSpec((chunk, dv), lambda s, h, c: (rows_out(s, h, c), h)),
                   pl.BlockSpec((1, 1, dk, dv), lambda s, h, c: (s, h, 0, 0))],
        out_shape=[jax.ShapeDtypeStruct((streams * length, heads * dv), BF16),
                   jax.ShapeDtypeStruct((streams, heads, dk, dv), F32)],
        scratch_shapes=[pltpu.VMEM((dk, dv), F32)],
        compiler_params=_params("parallel", "parallel", "arbitrary"),
        name=name,
    )(proj, proj, proj, proj, gk, s0, g_out.reshape(1, dv))


def _bias_lane_maps(heads):
    maps = np.zeros((heads, 3 * LANES, 2 * LANES), np.float32)
    for h in range(heads):
        for piece in range(3):
            maps[h, piece * LANES + h, 3 + piece] = -1.0
            maps[h, piece * LANES + h, LANES + piece] = 1.0
    const = np.zeros((1, 2 * LANES), np.float32)
    const[0, 0:3] = 1.0
    const[0, LANES + 3:LANES + 6] = 1.0
    return jnp.asarray(maps, BF16), jnp.asarray(const)


def _cumsum_bias_kernel(x_ref, maps_ref, const_ref, ka_ref, qa_ref, carry_scr, *, tile, heads):
    @pl.when(pl.program_id(0) == 0)
    def _():
        carry_scr[...] = jnp.zeros_like(carry_scr)

    r = lax.broadcasted_iota(jnp.int32, (tile, tile), 0)
    c = lax.broadcasted_iota(jnp.int32, (tile, tile), 1)
    x = x_ref[...]
    lower = jnp.where(r >= c, 1.0, 0.0).astype(BF16)
    cum = carry_scr[...] + _sum3(_dot, _split3(x), lower, other_first=True)
    carry_scr[...] += jnp.sum(x, axis=0, keepdims=True)
    pieces = jnp.concatenate(_split3(cum * LOG2E), axis=1)
    for h in range(heads):
        lanes = (_dot(pieces, maps_ref[h]) + const_ref[...]).astype(BF16)
        ka_ref[h] = lanes[:, :LANES]
        qa_ref[h] = lanes[:, LANES:]


def _cumsum_bias(x, *, length, heads, tile, name):
    maps, const = _bias_lane_maps(heads)
    out = jax.ShapeDtypeStruct((heads, length, LANES), BF16)
    return pl.pallas_call(
        functools.partial(_cumsum_bias_kernel, tile=tile, heads=heads),
        grid=(length // tile,),
        in_specs=[pl.BlockSpec((tile, LANES), lambda i: (i, 0)),
                  pl.BlockSpec(maps.shape, lambda i: (0, 0, 0)),
                  pl.BlockSpec(const.shape, lambda i: (0, 0))],
        out_specs=[pl.BlockSpec((heads, tile, LANES), lambda i: (0, i, 0)),
                   pl.BlockSpec((heads, tile, LANES), lambda i: (0, i, 0))],
        out_shape=[out, out],
        scratch_shapes=[pltpu.VMEM((1, LANES), F32)],
        compiler_params=_params("arbitrary"),
        name=name,
    )(x, maps, const)


def _cumsum_cached_kernel(past_ref, pastt_ref, new_ref, newt_ref, cq_ref, cpt_ref, cnt_ref, *, tile):
    past_t = pastt_ref[0]
    past_len = past_t.shape[1]
    r = lax.broadcasted_iota(jnp.int32, (tile, tile), 0)
    c = lax.broadcasted_iota(jnp.int32, (tile, tile), 1)
    upper = jnp.where(r <= c, 1.0, 0.0).astype(BF16)
    carry = jnp.zeros((past_t.shape[0], 1), F32)
    for t in range(past_len // tile):
        xt = past_t[:, t * tile:(t + 1) * tile]
        cpt_ref[0, :, t * tile:(t + 1) * tile] = carry + _sum3(_dot, _split3(xt), upper, other_first=False)
        carry = carry + jnp.sum(xt, axis=1, keepdims=True)
    n = newt_ref.shape[2]
    rn = lax.broadcasted_iota(jnp.int32, (n, n), 0)
    cn = lax.broadcasted_iota(jnp.int32, (n, n), 1)
    cnt_ref[0] = carry + _sum3(_dot, _split3(newt_ref[0]),
                               jnp.where(rn <= cn, 1.0, 0.0).astype(BF16), other_first=False)
    heads = past_t.shape[0]
    past_total = jnp.sum(past_ref[0], axis=0, keepdims=True)
    c_new = _sum3(_dot, _split3(new_ref[...]), jnp.where(rn >= cn, 1.0, 0.0).astype(BF16), other_first=True)
    cq_ref[0] = past_total + c_new[:, :heads]


def _cumsum_cached(past, past_t, new, new_t, *, row0, name):
    streams, past_len, heads = past.shape
    n = new_t.shape[2]
    base = row0 // n
    return pl.pallas_call(
        functools.partial(_cumsum_cached_kernel, tile=512),
        grid=(streams,),
        in_specs=[pl.BlockSpec((1, past_len, heads), lambda s: (s, 0, 0)),
                  pl.BlockSpec((1, heads, past_len), lambda s: (s, 0, 0)),
                  pl.BlockSpec((n, LANES), lambda s: (base + s, 0)),
                  pl.BlockSpec((1, heads, n), lambda s: (s, 0, 0))],
        out_specs=[pl.BlockSpec((1, n, heads), lambda s: (s, 0, 0)),
                   pl.BlockSpec((1, heads, past_len), lambda s: (s, 0, 0)),
                   pl.BlockSpec((1, heads, n), lambda s: (s, 0, 0))],
        out_shape=[jax.ShapeDtypeStruct((streams, n, heads), F32),
                   jax.ShapeDtypeStruct((streams, heads, past_len), F32),
                   jax.ShapeDtypeStruct((streams, heads, n), F32)],
        compiler_params=_params("parallel"),
        name=name,
    )(past, past_t, new, new_t)


def _fox_prompt_kernel(qi_ref, kj_ref, q_ref, qa_ref, k_ref, ka_ref, vt_ref, o_ref,
                       m_scr, acc_scr, p_scr, *, hp):
    step = pl.program_id(1)
    qi = qi_ref[step]
    kj = kj_ref[step]
    tq, tk = q_ref.shape[0], k_ref.shape[0]
    hd = q_ref.shape[1] // hp

    @pl.when(kj == 0)
    def _():
        m_scr[...] = jnp.full_like(m_scr, NEG_BIG)
        acc_scr[...] = jnp.zeros_like(acc_scr)

    units = [(h, c) for h in range(hp) for c in range(tq // QUERY_CHUNK)]

    def scores(unit, masked):
        h, c = unit
        sl = slice(h * hd, (h + 1) * hd)
        rows = slice(c * QUERY_CHUNK, (c + 1) * QUERY_CHUNK)
        keys = (c + 1) * QUERY_CHUNK if masked else tk
        q_aug = jnp.concatenate([q_ref[rows, sl], qa_ref[h, rows, :]], axis=1)
        k_aug = jnp.concatenate([k_ref[:keys, sl], ka_ref[h, :keys, :]], axis=1)
        s = _dot_nt(k_aug, q_aug)
        if masked:
            key = lax.broadcasted_iota(jnp.int32, s.shape, 0)
            qry = lax.broadcasted_iota(jnp.int32, s.shape, 1) + c * QUERY_CHUNK
            s = jnp.where(key <= qry, s, NEG_BIG)
        return s

    def update(masked):
        s_next = scores(units[0], masked)
        for i, (h, c) in enumerate(units):
            s = s_next
            if i + 1 < len(units):
                s_next = scores(units[i + 1], masked)
            keys = s.shape[0]
            cols = slice(c * QUERY_CHUNK, (c + 1) * QUERY_CHUNK)
            m_prev = m_scr[h, :, cols]
            m_new = jnp.maximum(m_prev, jnp.max(s, axis=0, keepdims=True))
            alpha = jnp.exp2(m_prev - m_new)
            m_rows = jnp.broadcast_to(m_new, (8, QUERY_CHUNK))
            for r0 in range(0, keys, EXP_ROWS):
                sc = s[r0:r0 + EXP_ROWS, :].reshape(EXP_ROWS // 8, 8, QUERY_CHUNK)
                p_scr[i % 2, r0:r0 + EXP_ROWS, :] = (
                    jnp.exp2(sc - m_rows).reshape(EXP_ROWS, QUERY_CHUNK).astype(BF16))
            vt_aug = jnp.concatenate([vt_ref[h, :, :keys], jnp.ones((SUM_ROWS, keys), BF16)], axis=0)
            acc_scr[h, :, cols] = alpha * acc_scr[h, :, cols] + _dot(vt_aug, p_scr[i % 2, :keys, :])
            m_scr[h, :, cols] = m_new

    @pl.when(kj < qi)
    def _():
        update(False)

    @pl.when(kj == qi)
    def _():
        update(True)
        for h in range(hp):
            acc = acc_scr[h]
            o_ref[:, h * hd:(h + 1) * hd] = (acc[:hd] / acc[hd:hd + 1]).T.astype(o_ref.dtype)


def _fox_prompt(q, k, v_t, ka, qa, *, length, tile, hp, name):
    heads, hd, _ = v_t.shape
    nq = length // tile
    pairs = [(i, j) for i in range(nq) for j in range(i + 1)]
    qi = jnp.asarray(np.array([p[0] for p in pairs], np.int32))
    kj = jnp.asarray(np.array([p[1] for p in pairs], np.int32))
    grid_spec = pltpu.PrefetchScalarGridSpec(
        num_scalar_prefetch=2,
        grid=(heads // hp, len(pairs)),
        in_specs=[pl.BlockSpec((tile, hp * hd), lambda g, s, qi, kj: (qi[s], g)),
                  pl.BlockSpec((hp, tile, LANES), lambda g, s, qi, kj: (g, qi[s], 0)),
                  pl.BlockSpec((tile, hp * hd), lambda g, s, qi, kj: (kj[s], g)),
                  pl.BlockSpec((hp, tile, LANES), lambda g, s, qi, kj: (g, kj[s], 0)),
                  pl.BlockSpec((hp, hd, tile), lambda g, s, qi, kj: (g, 0, kj[s]))],
        out_specs=pl.BlockSpec((tile, hp * hd), lambda g, s, qi, kj: (qi[s], g)),
        scratch_shapes=[pltpu.VMEM((hp, 1, tile), F32),
                        pltpu.VMEM((hp, hd + SUM_ROWS, tile), F32),
                        pltpu.VMEM((2, tile, QUERY_CHUNK), BF16)],
    )
    return pl.pallas_call(
        functools.partial(_fox_prompt_kernel, hp=hp),
        grid_spec=grid_spec,
        out_shape=jax.ShapeDtypeStruct((length, heads * hd), BF16),
        compiler_params=_params("parallel", "arbitrary"),
        name=name,
    )(qi, kj, q, qa, k, ka, v_t)


def _fox_cached_kernel(q_ref, kn_ref, vn_ref, kp_ref, vp_ref, cq_ref, cp_ref, cn_ref, o_ref,
                       m_scr, l_scr, acc_scr, *, heads, tp):
    t = pl.program_id(1)
    n = q_ref.shape[0]
    hd = q_ref.shape[1] // heads

    @pl.when(t == 0)
    def _():
        m_scr[...] = jnp.full_like(m_scr, NEG_BIG)
        l_scr[...] = jnp.zeros_like(l_scr)
        acc_scr[...] = jnp.zeros_like(acc_scr)

    def update(h, s, v):
        m_prev = m_scr[h]
        m_new = jnp.maximum(m_prev, jnp.max(s, axis=1, keepdims=True))
        alpha = jnp.exp2(m_prev - m_new)
        p = jnp.exp2(s - m_new)
        l_scr[h] = alpha * l_scr[h] + jnp.sum(p, axis=1, keepdims=True)
        acc_scr[h] = alpha * acc_scr[h] + _dot(p.astype(BF16), v)
        m_scr[h] = m_new

    for h in range(heads):
        sl = slice(h * hd, (h + 1) * hd)
        k = kp_ref[0, pl.ds(h, tp, stride=heads), :].astype(BF16)
        v = vp_ref[0, pl.ds(h, tp, stride=heads), :].astype(BF16)
        bias = (cq_ref[0][:, h:h + 1] - cp_ref[0, h]) * LOG2E
        update(h, _dot_nt(q_ref[:, sl], k) + bias, v)

    @pl.when(t == pl.num_programs(1) - 1)
    def _():
        row = lax.broadcasted_iota(jnp.int32, (n, n), 0)
        col = lax.broadcasted_iota(jnp.int32, (n, n), 1)
        for h in range(heads):
            sl = slice(h * hd, (h + 1) * hd)
            bias = (cq_ref[0][:, h:h + 1] - cn_ref[0, h]) * LOG2E
            s = jnp.where(col <= row, _dot_nt(q_ref[:, sl], kn_ref[:, sl]) + bias, NEG_BIG)
            update(h, s, vn_ref[:, sl])
            o_ref[:, sl] = (acc_scr[h] / l_scr[h]).astype(o_ref.dtype)


def _fox_cached(q, k_new, v_new, cache_k, cache_v, cq, cp_t, cn_t, *, row0, tp, name):
    streams, n, heads = cq.shape
    hd = cache_k.shape[2]
    past_len = cache_k.shape[1] // heads
    base = row0 // n
    return pl.pallas_call(
        functools.partial(_fox_cached_kernel, heads=heads, tp=tp),
        grid=(streams, past_len // tp),
        in_specs=[pl.BlockSpec((n, heads * hd), lambda s, t: (base + s, 0)),
                  pl.BlockSpec((n, heads * hd), lambda s, t: (base + s, 0)),
                  pl.BlockSpec((n, heads * hd), lambda s, t: (base + s, 0)),
                  pl.BlockSpec((1, tp * heads, hd), lambda s, t: (s, t, 0)),
                  pl.BlockSpec((1, tp * heads, hd), lambda s, t: (s, t, 0)),
                  pl.BlockSpec((1, n, heads), lambda s, t: (s, 0, 0)),
                  pl.BlockSpec((1, heads, 1, tp), lambda s, t: (s, 0, 0, t)),
                  pl.BlockSpec((1, heads, 1, n), lambda s, t: (s, 0, 0, 0))],
        out_specs=pl.BlockSpec((n, heads * hd), lambda s, t: (s, 0)),
        out_shape=jax.ShapeDtypeStruct((streams * n, heads * hd), BF16),
        scratch_shapes=[pltpu.VMEM((heads, n, 1), F32), pltpu.VMEM((heads, n, 1), F32),
                        pltpu.VMEM((heads, n, hd), F32)],
        compiler_params=_params("parallel", "arbitrary"),
        name=name,
    )(q, k_new, v_new, cache_k, cache_v, cq,
      cp_t.reshape(streams, heads, 1, past_len), cn_t.reshape(streams, heads, 1, n))


def kernel(x_prompt, x_sample, state_gla, cache_k, cache_v, cache_logf, g_mix, g_ffn, gla_w_in, gla_w_gk2, gla_b_gk, gla_g_out, gla_w_out, kv_g, kv_w, kv_b_f, kv_g_k, fox_w_q, fox_g_q, fox_w_o, ffn_w_in, ffn_w_out):
    batch, seq, d = x_prompt.shape
    dec_batch, dec_seq, _ = x_sample.shape
    assert batch == 1 and g_mix.shape[0] == 2 and state_gla.shape[0] == 1
    heads = FOX_HEADS
    hd = d // heads
    n_p = batch * seq
    gla_hk = gla_w_gk2.shape[2]
    gla_hv = gla_w_out.shape[1]
    gla_main = 2 * gla_hk + 2 * gla_hv
    gla_dk, gla_dv = gla_hk // GLA_HEADS, gla_hv // GLA_HEADS
    past_len = cache_k.shape[1]

    x = jnp.concatenate([x_prompt.reshape(n_p, d), x_sample.reshape(dec_batch * dec_seq, d)], axis=0)

    w_gla = gla_w_in[0]
    proj = _norm_matmul(x, g_mix[0], w_gla[:, :gla_main].astype(BF16), tn=1024, out_dtype=BF16, name="gla_proj")
    w_low = jnp.pad(w_gla[:, gla_main:], ((0, 0), (0, LANES - GLA_LOWRANK))).astype(BF16)
    w_gk2 = jnp.pad(gla_w_gk2[0], ((0, LANES - GLA_LOWRANK), (0, 0))).astype(BF16)
    gk = _gla_gate(x, g_mix[0], w_low, w_gk2, gla_b_gk[0], name="gla_gate")
    s0_prompt = jnp.zeros((batch, GLA_HEADS, gla_dk, gla_dv), F32)
    og_p, state_p = _gla_scan(proj, gk, s0_prompt, gla_g_out[0], row0=0, streams=batch, length=seq,
                              chunk=256, name="gla_scan_prompt")
    og_s, state_s = _gla_scan(proj, gk, state_gla[0], gla_g_out[0], row0=n_p, streams=dec_batch,
                              length=dec_seq, chunk=dec_seq, name="gla_scan_sample")
    og = jnp.concatenate([og_p, og_s], axis=0)
    h = _res_matmul(og, gla_w_out[0].astype(BF16), x, tn=1024, name="gla_out")

    act = _ffn_in(h, g_ffn[0], ffn_w_in[0].astype(BF16), tn=512, name="ffn0_in")
    h = _res_matmul(act, ffn_w_out[0].astype(BF16), h, tn=512, name="ffn0_out")

    k32, k16 = _head_proj(h, kv_g, kv_w[:, :d].astype(BF16), kv_g_k, head_dim=hd, head_norm=True,
                          scale=1.0, emit_f32=True, emit_t=False, tn=1024, name="kv_k")
    v32, v16, v16_t = _head_proj(h, kv_g, kv_w[:, d:2 * d].astype(BF16), kv_g_k, head_dim=hd, head_norm=False,
                                 scale=1.0, emit_f32=True, emit_t=True, tn=1024, name="kv_v")
    (q16,) = _head_proj(h, g_mix[1], fox_w_q[0].astype(BF16), fox_g_q[0], head_dim=hd, head_norm=True,
                        scale=hd ** -0.5 * LOG2E, emit_f32=False, emit_t=False, tn=1024, name="fox_q")
    w_f = jnp.pad(kv_w[:, 2 * d:], ((0, 0), (0, LANES - heads))).astype(BF16)
    b_f = jnp.pad(kv_b_f, (0, LANES - heads))
    logf, logf_t = _logf_proj(h, kv_g, w_f, b_f, name="kv_logf")

    ka, qa = _cumsum_bias(logf, length=n_p, heads=heads, tile=512, name="logf_cumsum_prompt")
    attn_p = _fox_prompt(q16, k16, v16_t, ka, qa, length=n_p, tile=1024, hp=2, name="fox_attn_prompt")
    new_t = logf_t[:heads, n_p:].reshape(heads, dec_batch, dec_seq).transpose(1, 0, 2)
    cq_s, cp_t, cn_t = _cumsum_cached(cache_logf, cache_logf.transpose(0, 2, 1), logf, new_t,
                                      row0=n_p, name="logf_cumsum_sample")
    attn_s = _fox_cached(q16, k16, v16, cache_k.reshape(dec_batch, past_len * heads, hd),
                         cache_v.reshape(dec_batch, past_len * heads, hd), cq_s, cp_t, cn_t,
                         row0=n_p, tp=1024, name="fox_attn_sample")
    attn = jnp.concatenate([attn_p, attn_s], axis=0)
    h = _res_matmul(attn, fox_w_o[0].astype(BF16), h, tn=1024, name="fox_out")

    act = _ffn_in(h, g_ffn[1], ffn_w_in[1].astype(BF16), tn=512, name="ffn1_in")
    y = _res_matmul(act, ffn_w_out[1].astype(BF16), h, tn=512, name="ffn1_out")

    def split(a, tail):
        return a[:n_p].reshape((batch, seq) + tail), a[n_p:].reshape((dec_batch, dec_seq) + tail)

    y_p, y_s = split(y, (d,))
    k_p, k_s = split(k32, (heads, hd))
    v_p, v_s = split(v32, (heads, hd))
    lf_p, lf_s = split(logf[:, :heads], (heads,))
    return (y_p, y_s, state_p[None], k_p, v_p, lf_p, state_s[None], k_s, v_s, lf_s)
```

```python
import functools
import math

import jax
import jax.numpy as jnp
import numpy as np
from jax import lax
from jax.experimental import pallas as pl
from jax.experimental.pallas import tpu as pltpu

F32 = jnp.float32
BF16 = jnp.bfloat16

EPS = 1e-6
GLA_HEADS = 4
GLA_TAU = 16.0
GLA_LOWRANK = 16
FOX_HEADS = 16
LANES = 128
ROW_TILE = 512
VMEM_LIMIT = 48 * 1024 * 1024
VMEM_LIMIT_WIDE_K = 56 * 1024 * 1024
NEG_BIG = -1e30
QUERY_CHUNK = 256
EXP_ROWS = 128
SUM_ROWS = 16
LOG2E = math.log2(math.e)


def _params(*sem, vmem=VMEM_LIMIT):
    return pltpu.CompilerParams(dimension_semantics=sem, vmem_limit_bytes=vmem)


def _dot(a, b):
    return jnp.dot(a, b, preferred_element_type=F32)


def _dot_nt(a, b):
    return lax.dot_general(a, b, (((1,), (1,)), ((), ())), preferred_element_type=F32)


def _dot_tn(a, b):
    return lax.dot_general(a, b, (((0,), (0,)), ((), ())), preferred_element_type=F32)


def _split3(x):
    hi = x.astype(BF16)
    r1 = x - hi.astype(F32)
    mid = r1.astype(BF16)
    lo = (r1 - mid.astype(F32)).astype(BF16)
    return hi, mid, lo


def _sum3(dot, pieces, other, other_first):
    hi, mid, lo = pieces
    if other_first:
        return (dot(other, lo) + dot(other, mid)) + dot(other, hi)
    return (dot(lo, other) + dot(mid, other)) + dot(hi, other)


def _inv_rms(x):
    return lax.rsqrt(jnp.mean(x * x, axis=-1, keepdims=True) + EPS)


def _rms(x, g):
    return x * _inv_rms(x) * g


def _log_sigmoid(x):
    return jnp.minimum(x, 0.0) - jnp.log1p(jnp.exp(-jnp.abs(x)))


def _silu(x):
    return x / (1.0 + jnp.exp(-x))


def _pair_specs(n_p, cols, col_map):
    return [pl.BlockSpec((ROW_TILE, cols), lambda j, i: (jnp.minimum(i, n_p - 1), col_map(j))),
            pl.BlockSpec((ROW_TILE, cols), lambda j, i: (0, col_map(j)))]


def _pair_load(p_ref, s_ref, n_p):
    return jnp.where(pl.program_id(1) < n_p, p_ref[...], s_ref[...])


def _pair_store(p_ref, s_ref, n_p, val, cols=slice(None)):
    i = pl.program_id(1)

    @pl.when(i < n_p)
    def _():
        p_ref[:, cols] = val

    @pl.when(i >= n_p)
    def _():
        s_ref[:, cols] = val


def _cast_weight(w_scr, w_ref, gain_ref=None):
    @pl.when(pl.program_id(1) == 0)
    def _():
        w = w_ref[...]
        if gain_ref is not None:
            w = w * gain_ref[...]
        w_scr[...] = w.astype(BF16)


def _x_proj_kernel(xp_ref, xs_ref, g_ref, w_ref, o_ref, w_scr, *, n_p):
    _cast_weight(w_scr, w_ref, g_ref)
    x = _pair_load(xp_ref, xs_ref, n_p)
    o_ref[...] = (_inv_rms(x) * _dot(x.astype(BF16), w_scr[...])).astype(o_ref.dtype)


def _x_proj(x_pair, g, w, *, n_cols, tn, name):
    xp, xs = x_pair
    d = xp.shape[1]
    n_p = xp.shape[0] // ROW_TILE
    m = xp.shape[0] + xs.shape[0]
    return pl.pallas_call(
        functools.partial(_x_proj_kernel, n_p=n_p),
        grid=(n_cols // tn, n_p + 1),
        in_specs=_pair_specs(n_p, d, lambda j: 0) + [
            pl.BlockSpec((d, 1), lambda j, i: (0, 0)),
            pl.BlockSpec((d, tn), lambda j, i: (0, j))],
        out_specs=pl.BlockSpec((ROW_TILE, tn), lambda j, i: (i, j)),
        out_shape=jax.ShapeDtypeStruct((m, n_cols), BF16),
        scratch_shapes=[pltpu.VMEM((d, tn), BF16)],
        compiler_params=_params("parallel", "arbitrary"),
        name=name,
    )(xp, xs, g.reshape(d, 1), w)


def _head_proj_kernel(a_ref, g_ref, w_ref, hg_ref, *rest, n_p, head_dim, head_norm, scale, emit_f32, emit_t):
    outs, w_scr = rest[:-1], rest[-1]
    _cast_weight(w_scr, w_ref, g_ref)
    a = a_ref[...]
    af = a.astype(F32)
    acc = _inv_rms(af) * _dot(a, w_scr[...])
    n_f32 = 2 * int(emit_f32)
    ys = []
    for h in range(acc.shape[1] // head_dim):
        y = acc[:, h * head_dim:(h + 1) * head_dim]
        if head_norm:
            y = _rms(y, hg_ref[...])
        if scale != 1.0:
            y = y * scale
        if emit_t:
            outs[n_f32 + 1][h] = y.T.astype(BF16)
        ys.append(y)
    y = jnp.concatenate(ys, axis=1)
    outs[n_f32][...] = y.astype(BF16)
    if emit_f32:
        _pair_store(outs[0], outs[1], n_p, y)


def _head_proj(a, g, w, head_gain, *, col0, n_p, head_dim, head_norm, scale, emit_f32, emit_t, tn, name):
    m, d = a.shape
    n = d
    jb = col0 // tn
    out_spec = pl.BlockSpec((ROW_TILE, tn), lambda j, i: (i, j))
    specs, shapes = [out_spec], [jax.ShapeDtypeStruct((m, n), BF16)]
    if emit_f32:
        specs = _pair_specs(n_p, tn, lambda j: j) + specs
        shapes = [jax.ShapeDtypeStruct((n_p * ROW_TILE, n), F32),
                  jax.ShapeDtypeStruct((m - n_p * ROW_TILE, n), F32)] + shapes
    if emit_t:
        specs.append(pl.BlockSpec((tn // head_dim, head_dim, ROW_TILE), lambda j, i: (j, 0, i)))
        shapes.append(jax.ShapeDtypeStruct((n // head_dim, head_dim, m), BF16))
    return pl.pallas_call(
        functools.partial(_head_proj_kernel, n_p=n_p, head_dim=head_dim, head_norm=head_norm,
                          scale=scale, emit_f32=emit_f32, emit_t=emit_t),
        grid=(n // tn, m // ROW_TILE),
        in_specs=[pl.BlockSpec((ROW_TILE, d), lambda j, i: (i, 0)),
                  pl.BlockSpec((d, 1), lambda j, i: (0, 0)),
                  pl.BlockSpec((d, tn), lambda j, i: (0, jb + j)),
                  pl.BlockSpec((1, head_dim), lambda j, i: (0, 0))],
        out_specs=specs,
        out_shape=shapes,
        scratch_shapes=[pltpu.VMEM((d, tn), BF16)],
        compiler_params=_params("parallel", "arbitrary"),
        name=name,
    )(a, g.reshape(d, 1), w, head_gain.reshape(1, head_dim))


def _ffn_in_kernel(a_ref, g_ref, wg_ref, wu_ref, o_ref, wg_scr, wu_scr):
    _cast_weight(wg_scr, wg_ref, g_ref)
    _cast_weight(wu_scr, wu_ref, g_ref)
    a = a_ref[...]
    af = a.astype(F32)
    r = _inv_rms(af)
    gate = r * _dot(a, wg_scr[...])
    up = r * _dot(a, wu_scr[...])
    o_ref[...] = (_silu(gate) * up).astype(o_ref.dtype)


def _ffn_in(a, g, w_in, *, tn, name):
    m, d = a.shape
    ff = w_in.shape[1] // 2
    nj = ff // tn
    return pl.pallas_call(
        _ffn_in_kernel,
        grid=(nj, m // ROW_TILE),
        in_specs=[pl.BlockSpec((ROW_TILE, d), lambda j, i: (i, 0)),
                  pl.BlockSpec((d, 1), lambda j, i: (0, 0)),
                  pl.BlockSpec((d, tn), lambda j, i: (0, j)),
                  pl.BlockSpec((d, tn), lambda j, i: (0, j + nj))],
        out_specs=pl.BlockSpec((ROW_TILE, tn), lambda j, i: (i, j)),
        out_shape=jax.ShapeDtypeStruct((m, ff), BF16),
        scratch_shapes=[pltpu.VMEM((d, tn), BF16), pltpu.VMEM((d, tn), BF16)],
        compiler_params=_params("parallel", "arbitrary"),
        name=name,
    )(a, g.reshape(d, 1), w_in, w_in)


def _res_kernel(a_ref, w_ref, *rest, n_p, res_pair, out_pair):
    w_scr = rest[-1]
    n_res = 2 if res_pair else 1
    res_refs, out_refs = rest[:n_res], rest[n_res:-1]
    _cast_weight(w_scr, w_ref)
    res = _pair_load(res_refs[0], res_refs[1], n_p) if res_pair else res_refs[0][...]
    y = res + _dot(a_ref[...], w_scr[...])
    if out_pair:
        _pair_store(out_refs[0], out_refs[1], n_p, y)
    else:
        out_refs[0][...] = y
        out_refs[1][...] = y.astype(BF16)


def _res_matmul(a, w, res, *, n_p, tn, out_pair, name, vmem=VMEM_LIMIT):
    m, k = a.shape
    n = w.shape[1]
    res_pair = isinstance(res, tuple)
    tile_spec = pl.BlockSpec((ROW_TILE, tn), lambda j, i: (i, j))
    res_specs = _pair_specs(n_p, tn, lambda j: j) if res_pair else [tile_spec]
    if out_pair:
        out_specs = _pair_specs(n_p, tn, lambda j: j)
        out_shape = [jax.ShapeDtypeStruct((n_p * ROW_TILE, n), F32),
                     jax.ShapeDtypeStruct((m - n_p * ROW_TILE, n), F32)]
    else:
        out_specs = [tile_spec, tile_spec]
        out_shape = [jax.ShapeDtypeStruct((m, n), F32), jax.ShapeDtypeStruct((m, n), BF16)]
    return pl.pallas_call(
        functools.partial(_res_kernel, n_p=n_p, res_pair=res_pair, out_pair=out_pair),
        grid=(n // tn, m // ROW_TILE),
        in_specs=[pl.BlockSpec((ROW_TILE, k), lambda j, i: (i, 0)),
                  pl.BlockSpec((k, tn), lambda j, i: (0, j))] + res_specs,
        out_specs=out_specs,
        out_shape=out_shape,
        scratch_shapes=[pltpu.VMEM((k, tn), BF16)],
        compiler_params=_params("parallel", "arbitrary", vmem=vmem),
        name=name,
    )(a, w, *(res if res_pair else (res,)))


def _gla_gate_kernel(xp_ref, xs_ref, g_ref, wl_ref, w2_ref, b_ref, o_ref, *, n_p):
    x = jnp.where(pl.program_id(0) < n_p, xp_ref[...], xs_ref[...])
    u = _rms(x, g_ref[...]).astype(BF16)
    low = _dot(u, wl_ref[...]).astype(BF16)
    o_ref[...] = _log_sigmoid(_dot(low, w2_ref[...]) + b_ref[...]) * (1.0 / GLA_TAU)


def _gla_gate(x_pair, g, w_low, w_gk2, b_gk, *, name):
    xp, xs = x_pair
    d = xp.shape[1]
    n_p = xp.shape[0] // ROW_TILE
    m = xp.shape[0] + xs.shape[0]
    n = w_gk2.shape[1]
    return pl.pallas_call(
        functools.partial(_gla_gate_kernel, n_p=n_p),
        grid=(n_p + 1,),
        in_specs=[pl.BlockSpec((ROW_TILE, d), lambda i: (jnp.minimum(i, n_p - 1), 0)),
                  pl.BlockSpec((ROW_TILE, d), lambda i: (0, 0)),
                  pl.BlockSpec((1, d), lambda i: (0, 0)),
                  pl.BlockSpec((d, LANES), lambda i: (0, 0)),
                  pl.BlockSpec((LANES, n), lambda i: (0, 0)),
                  pl.BlockSpec((1, n), lambda i: (0, 0))],
        out_specs=pl.BlockSpec((ROW_TILE, n), lambda i: (i, 0)),
        out_shape=jax.ShapeDtypeStruct((m, n), F32),
        compiler_params=_params("arbitrary"),
        name=name,
    )(xp, xs, g.reshape(1, d), w_low, w_gk2, b_gk.reshape(1, n))


def _logf_kernel(x_ref, g_ref, w_ref, wt_ref, b_ref, bt_ref, o_ref, ot_ref):
    u = _rms(x_ref[...], g_ref[...]).astype(BF16)
    o_ref[...] = _log_sigmoid(_dot(u, w_ref[...]) + b_ref[...])
    ot_ref[...] = _log_sigmoid(_dot_nt(wt_ref[...], u) + bt_ref[...])


def _logf_proj(x, g, w_f, b_f, *, name):
    m, d = x.shape
    return pl.pallas_call(
        _logf_kernel,
        grid=(m // ROW_TILE,),
        in_specs=[pl.BlockSpec((ROW_TILE, d), lambda i: (i, 0)),
                  pl.BlockSpec((1, d), lambda i: (0, 0)),
                  pl.BlockSpec((d, LANES), lambda i: (0, 0)),
                  pl.BlockSpec((LANES, d), lambda i: (0, 0)),
                  pl.BlockSpec((1, LANES), lambda i: (0, 0)),
                  pl.BlockSpec((LANES, 1), lambda i: (0, 0))],
        out_specs=[pl.BlockSpec((ROW_TILE, LANES), lambda i: (i, 0)),
                   pl.BlockSpec((LANES, ROW_TILE), lambda i: (0, i))],
        out_shape=[jax.ShapeDtypeStruct((m, LANES), F32),
                   jax.ShapeDtypeStruct((LANES, m), F32)],
        compiler_params=_params("parallel"),
        name=name,
    )(x, g.reshape(1, d), w_f, w_f.T, b_f.reshape(1, LANES), b_f.reshape(LANES, 1))


def _block_ref_rows(b, h):
    c, d = b.shape
    span = 2 * h
    if span >= 8:
        b3 = b.reshape(c // span, span, d)
        return jnp.broadcast_to(b3[:, h:h + 1, :], b3.shape).reshape(c, d)
    b3 = b.reshape(c // 8, 8, d)
    sub = lax.broadcasted_iota(jnp.int32, b3.shape, 1)
    ref = jnp.broadcast_to(b3[:, h:h + 1, :], b3.shape)
    for blk in range(1, 8 // span):
        cand = jnp.broadcast_to(b3[:, blk * span + h:blk * span + h + 1, :], b3.shape)
        ref = jnp.where(sub >= blk * span, cand, ref)
    return ref.reshape(c, d)


def _gla_kernel(q_ref, k_ref, v_ref, gate_ref, gk_ref, s0_ref, gout_ref, *rest, chunk):
    o_ref, sn_ref, s_scr = rest[-3:]
    c_idx = pl.program_id(2)

    @pl.when(c_idx == 0)
    def _():
        s_scr[...] = s0_ref[0, 0]

    dk = q_ref.shape[1]
    q = q_ref[...].astype(F32) * (dk ** -0.5)
    k = k_ref[...].astype(F32)
    v = v_ref[...]
    row = lax.broadcasted_iota(jnp.int32, (chunk, chunk), 0)
    col = lax.broadcasted_iota(jnp.int32, (chunk, chunk), 1)
    tri = jnp.where(row >= col, 1.0, 0.0).astype(BF16)
    g3 = _split3(gk_ref[...])
    b = _sum3(_dot, g3, tri, other_first=True)
    b_end = b[chunk - 1:chunk, :]
    b_end_col = _sum3(_dot_tn, g3, jnp.ones((chunk, LANES), BF16), other_first=False)

    s_prev = s_scr[...]
    o = _dot((q * jnp.exp(b)).astype(BF16), s_prev.astype(BF16))

    att = jnp.where(row == col, _dot_nt(q.astype(BF16), k.astype(BF16)), 0.0)
    split_bit = jnp.where(row > col, row ^ col, 0)
    sub_row = lax.broadcasted_iota(jnp.int32, (chunk, dk), 0)
    h = 1
    while h < chunk:
        f = jnp.exp(-jnp.abs(b - _block_ref_rows(b, h)))
        z = (jnp.where((sub_row & h) != 0, q, k) * f).astype(BF16)
        att = jnp.where(split_bit >= h, _dot_nt(z, z), att)
        h *= 2
    o = o + _dot(att.astype(BF16), v)

    k_dec = (k * jnp.exp(b_end - b)).astype(BF16)
    decay = jnp.exp(b_end_col)
    decay = jnp.concatenate([decay] * (s_prev.shape[1] // LANES), axis=1)
    s_new = decay * s_prev + _dot_tn(k_dec, v)
    s_scr[...] = s_new

    @pl.when(c_idx == pl.num_programs(2) - 1)
    def _():
        sn_ref[0, 0] = s_new

    o_ref[...] = (_rms(o, gout_ref[...]) * _silu(gate_ref[...].astype(F32))).astype(o_ref.dtype)


def _gla_scan(proj, gk, s0, g_out, *, row0, streams, length, chunk, out_buf, name):
    heads = GLA_HEADS
    dk = gk.shape[1] // heads
    dv = s0.shape[-1]
    n_chunks = length // chunk
    base = row0 // chunk
    kq, vq = dk * heads // dk, 2 * dk * heads // dv

    def rows(s, h, c):
        return base + s * n_chunks + c

    in_specs = [pl.BlockSpec((chunk, dk), lambda s, h, c: (rows(s, h, c), h)),
                pl.BlockSpec((chunk, dk), lambda s, h, c: (rows(s, h, c), kq + h)),
                pl.BlockSpec((chunk, dv), lambda s, h, c: (rows(s, h, c), vq + h)),
                pl.BlockSpec((chunk, dv), lambda s, h, c: (rows(s, h, c), vq + heads + h)),
                pl.BlockSpec((chunk, dk), lambda s, h, c: (rows(s, h, c), h)),
                pl.BlockSpec((1, 1, dk, dv), lambda s, h, c: (s, h, 0, 0)),
                pl.BlockSpec((1, dv), lambda s, h, c: (0, 0))]
    args = [proj, proj, proj, proj, gk, s0, g_out.reshape(1, dv)]
    aliases = {}
    if out_buf is not None:
        in_specs.append(pl.BlockSpec(memory_space=pl.ANY))
        args.append(out_buf)
        aliases = {len(args) - 1: 0}
    return pl.pallas_call(
        functools.partial(_gla_kernel, chunk=chunk),
        grid=(streams, heads, n_chunks),
        in_specs=in_specs,
        out_specs=[pl.BlockSpec((chunk, dv), lambda s, h, c: (rows(s, h, c), h)),
                   pl.BlockSpec((1, 1, dk, dv), lambda s, h, c: (s, h, 0, 0))],
        out_shape=[jax.ShapeDtypeStruct((proj.shape[0], heads * dv), BF16),
                   jax.ShapeDtypeStruct((streams, heads, dk, dv), F32)],
        scratch_shapes=[pltpu.VMEM((dk, dv), F32)],
        input_output_aliases=aliases,
        compiler_params=_params("parallel", "parallel", "arbitrary"),
        name=name,
    )(*args)


def _bias_lane_maps(heads):
    maps = np.zeros((heads, 3 * LANES, 2 * LANES), np.float32)
    for h in range(heads):
        for piece in range(3):
            maps[h, piece * LANES + h, 3 + piece] = -1.0
            maps[h, piece * LANES + h, LANES + piece] = 1.0
    const = np.zeros((1, 2 * LANES), np.float32)
    const[0, 0:3] = 1.0
    const[0, LANES + 3:LANES + 6] = 1.0
    return jnp.asarray(maps, BF16), jnp.asarray(const)


def _cumsum_bias_kernel(x_ref, maps_ref, const_ref, ka_ref, qa_ref, carry_scr, *, tile, heads):
    @pl.when(pl.program_id(0) == 0)
    def _():
        carry_scr[...] = jnp.zeros_like(carry_scr)

    r = lax.broadcasted_iota(jnp.int32, (tile, tile), 0)
    c = lax.broadcasted_iota(jnp.int32, (tile, tile), 1)
    x = x_ref[...]
    lower = jnp.where(r >= c, 1.0, 0.0).astype(BF16)
    cum = carry_scr[...] + _sum3(_dot, _split3(x), lower, other_first=True)
    carry_scr[...] += jnp.sum(x, axis=0, keepdims=True)
    pieces = jnp.concatenate(_split3(cum * LOG2E), axis=1)
    for h in range(heads):
        lanes = (_dot(pieces, maps_ref[h]) + const_ref[...]).astype(BF16)
        ka_ref[h] = lanes[:, :LANES]
        qa_ref[h] = lanes[:, LANES:]


def _cumsum_bias(x, *, length, heads, tile, name):
    maps, const = _bias_lane_maps(heads)
    out = jax.ShapeDtypeStruct((heads, length, LANES), BF16)
    return pl.pallas_call(
        functools.partial(_cumsum_bias_kernel, tile=tile, heads=heads),
        grid=(length // tile,),
        in_specs=[pl.BlockSpec((tile, LANES), lambda i: (i, 0)),
                  pl.BlockSpec(maps.shape, lambda i: (0, 0, 0)),
                  pl.BlockSpec(const.shape, lambda i: (0, 0))],
        out_specs=[pl.BlockSpec((heads, tile, LANES), lambda i: (0, i, 0)),
                   pl.BlockSpec((heads, tile, LANES), lambda i: (0, i, 0))],
        out_shape=[out, out],
        scratch_shapes=[pltpu.VMEM((1, LANES), F32)],
        compiler_params=_params("arbitrary"),
        name=name,
    )(x, maps, const)


def _cumsum_cached_kernel(past_ref, pastt_ref, new_ref, newt_ref, cq_ref, cpt_ref, cnt_ref, *, tile):
    past_t = pastt_ref[0]
    past_len = past_t.shape[1]
    r = lax.broadcasted_iota(jnp.int32, (tile, tile), 0)
    c = lax.broadcasted_iota(jnp.int32, (tile, tile), 1)
    upper = jnp.where(r <= c, 1.0, 0.0).astype(BF16)
    carry = jnp.zeros((past_t.shape[0], 1), F32)
    for t in range(past_len // tile):
        xt = past_t[:, t * tile:(t + 1) * tile]
        cpt_ref[0, :, t * tile:(t + 1) * tile] = carry + _sum3(_dot, _split3(xt), upper, other_first=False)
        carry = carry + jnp.sum(xt, axis=1, keepdims=True)
    n = newt_ref.shape[2]
    rn = lax.broadcasted_iota(jnp.int32, (n, n), 0)
    cn = lax.broadcasted_iota(jnp.int32, (n, n), 1)
    cnt_ref[0] = carry + _sum3(_dot, _split3(newt_ref[0]),
                               jnp.where(rn <= cn, 1.0, 0.0).astype(BF16), other_first=False)
    heads = past_t.shape[0]
    past_total = jnp.sum(past_ref[0], axis=0, keepdims=True)
    c_new = _sum3(_dot, _split3(new_ref[...]), jnp.where(rn >= cn, 1.0, 0.0).astype(BF16), other_first=True)
    cq_ref[0] = past_total + c_new[:, :heads]


def _cumsum_cached(past, past_t, new, new_t, *, row0, name):
    streams, past_len, heads = past.shape
    n = new_t.shape[2]
    base = row0 // n
    return pl.pallas_call(
        functools.partial(_cumsum_cached_kernel, tile=512),
        grid=(streams,),
        in_specs=[pl.BlockSpec((1, past_len, heads), lambda s: (s, 0, 0)),
                  pl.BlockSpec((1, heads, past_len), lambda s: (s, 0, 0)),
                  pl.BlockSpec((n, LANES), lambda s: (base + s, 0)),
                  pl.BlockSpec((1, heads, n), lambda s: (s, 0, 0))],
        out_specs=[pl.BlockSpec((1, n, heads), lambda s: (s, 0, 0)),
                   pl.BlockSpec((1, heads, past_len), lambda s: (s, 0, 0)),
                   pl.BlockSpec((1, heads, n), lambda s: (s, 0, 0))],
        out_shape=[jax.ShapeDtypeStruct((streams, n, heads), F32),
                   jax.ShapeDtypeStruct((streams, heads, past_len), F32),
                   jax.ShapeDtypeStruct((streams, heads, n), F32)],
        compiler_params=_params("parallel"),
        name=name,
    )(past, past_t, new, new_t)


def _fox_prompt_kernel(qi_ref, kj_ref, q_ref, qa_ref, k_ref, ka_ref, vt_ref, o_ref,
                       m_scr, acc_scr, p_scr, *, hp):
    step = pl.program_id(1)
    qi = qi_ref[step]
    kj = kj_ref[step]
    tq, tk = q_ref.shape[0], k_ref.shape[0]
    hd = q_ref.shape[1] // hp

    @pl.when(kj == 0)
    def _():
        m_scr[...] = jnp.full_like(m_scr, NEG_BIG)
        acc_scr[...] = jnp.zeros_like(acc_scr)

    units = [(h, c) for h in range(hp) for c in range(tq // QUERY_CHUNK)]

    def scores(unit, masked):
        h, c = unit
        sl = slice(h * hd, (h + 1) * hd)
        rows = slice(c * QUERY_CHUNK, (c + 1) * QUERY_CHUNK)
        keys = (c + 1) * QUERY_CHUNK if masked else tk
        q_aug = jnp.concatenate([q_ref[rows, sl], qa_ref[h, rows, :]], axis=1)
        k_aug = jnp.concatenate([k_ref[:keys, sl], ka_ref[h, :keys, :]], axis=1)
        s = _dot_nt(k_aug, q_aug)
        if masked:
            key = lax.broadcasted_iota(jnp.int32, s.shape, 0)
            qry = lax.broadcasted_iota(jnp.int32, s.shape, 1) + c * QUERY_CHUNK
            s = jnp.where(key <= qry, s, NEG_BIG)
        return s

    def update(masked):
        s_next = scores(units[0], masked)
        for i, (h, c) in enumerate(units):
            s = s_next
            if i + 1 < len(units):
                s_next = scores(units[i + 1], masked)
            keys = s.shape[0]
            cols = slice(c * QUERY_CHUNK, (c + 1) * QUERY_CHUNK)
            m_prev = m_scr[h, :, cols]
            m_new = jnp.maximum(m_prev, jnp.max(s, axis=0, keepdims=True))
            alpha = jnp.exp2(m_prev - m_new)
            m_rows = jnp.broadcast_to(m_new, (8, QUERY_CHUNK))
            for r0 in range(0, keys, EXP_ROWS):
                sc = s[r0:r0 + EXP_ROWS, :].reshape(EXP_ROWS // 8, 8, QUERY_CHUNK)
                p_scr[i % 2, r0:r0 + EXP_ROWS, :] = (
                    jnp.exp2(sc - m_rows).reshape(EXP_ROWS, QUERY_CHUNK).astype(BF16))
            vt_aug = jnp.concatenate([vt_ref[h, :, :keys], jnp.ones((SUM_ROWS, keys), BF16)], axis=0)
            acc_scr[h, :, cols] = alpha * acc_scr[h, :, cols] + _dot(vt_aug, p_scr[i % 2, :keys, :])
            m_scr[h, :, cols] = m_new

    @pl.when(kj < qi)
    def _():
        update(False)

    @pl.when(kj == qi)
    def _():
        update(True)
        for h in range(hp):
            acc = acc_scr[h]
            o_ref[:, h * hd:(h + 1) * hd] = (acc[:hd] / acc[hd:hd + 1]).T.astype(o_ref.dtype)


def _fox_prompt(q, k, v_t, ka, qa, *, length, tile, hp, name):
    heads, hd, _ = v_t.shape
    nq = length // tile
    pairs = [(i, j) for i in range(nq) for j in range(i + 1)]
    qi = jnp.asarray(np.array([p[0] for p in pairs], np.int32))
    kj = jnp.asarray(np.array([p[1] for p in pairs], np.int32))
    grid_spec = pltpu.PrefetchScalarGridSpec(
        num_scalar_prefetch=2,
        grid=(heads // hp, len(pairs)),
        in_specs=[pl.BlockSpec((tile, hp * hd), lambda g, s, qi, kj: (qi[s], g)),
                  pl.BlockSpec((hp, tile, LANES), lambda g, s, qi, kj: (g, qi[s], 0)),
                  pl.BlockSpec((tile, hp * hd), lambda g, s, qi, kj: (kj[s], g)),
                  pl.BlockSpec((hp, tile, LANES), lambda g, s, qi, kj: (g, kj[s], 0)),
                  pl.BlockSpec((hp, hd, tile), lambda g, s, qi, kj: (g, 0, kj[s]))],
        out_specs=pl.BlockSpec((tile, hp * hd), lambda g, s, qi, kj: (qi[s], g)),
        scratch_shapes=[pltpu.VMEM((hp, 1, tile), F32),
                        pltpu.VMEM((hp, hd + SUM_ROWS, tile), F32),
                        pltpu.VMEM((2, tile, QUERY_CHUNK), BF16)],
    )
    return pl.pallas_call(
        functools.partial(_fox_prompt_kernel, hp=hp),
        grid_spec=grid_spec,
        out_shape=jax.ShapeDtypeStruct((q.shape[0], heads * hd), BF16),
        compiler_params=_params("parallel", "arbitrary"),
        name=name,
    )(qi, kj, q, qa, k, ka, v_t)


def _fox_cached_kernel(q_ref, kn_ref, vn_ref, kp_ref, vp_ref, cq_ref, cp_ref, cn_ref, buf_ref, o_ref,
                       m_scr, l_scr, acc_scr, *, heads, tp):
    t = pl.program_id(1)
    n = q_ref.shape[0]
    hd = q_ref.shape[1] // heads

    @pl.when(t == 0)
    def _():
        m_scr[...] = jnp.full_like(m_scr, NEG_BIG)
        l_scr[...] = jnp.zeros_like(l_scr)
        acc_scr[...] = jnp.zeros_like(acc_scr)

    def update(h, s, v):
        m_prev = m_scr[h]
        m_new = jnp.maximum(m_prev, jnp.max(s, axis=1, keepdims=True))
        alpha = jnp.exp2(m_prev - m_new)
        p = jnp.exp2(s - m_new)
        l_scr[h] = alpha * l_scr[h] + jnp.sum(p, axis=1, keepdims=True)
        acc_scr[h] = alpha * acc_scr[h] + _dot(p.astype(BF16), v)
        m_scr[h] = m_new

    for h in range(heads):
        sl = slice(h * hd, (h + 1) * hd)
        k = kp_ref[0, pl.ds(h, tp, stride=heads), :].astype(BF16)
        v = vp_ref[0, pl.ds(h, tp, stride=heads), :].astype(BF16)
        bias = (cq_ref[0][:, h:h + 1] - cp_ref[0, h]) * LOG2E
        update(h, _dot_nt(q_ref[:, sl], k) + bias, v)

    @pl.when(t == pl.num_programs(1) - 1)
    def _():
        row = lax.broadcasted_iota(jnp.int32, (n, n), 0)
        col = lax.broadcasted_iota(jnp.int32, (n, n), 1)
        for h in range(heads):
            sl = slice(h * hd, (h + 1) * hd)
            bias = (cq_ref[0][:, h:h + 1] - cn_ref[0, h]) * LOG2E
            s = jnp.where(col <= row, _dot_nt(q_ref[:, sl], kn_ref[:, sl]) + bias, NEG_BIG)
            update(h, s, vn_ref[:, sl])
            o_ref[:, sl] = (acc_scr[h] / l_scr[h]).astype(o_ref.dtype)


def _fox_cached(q, k_new, v_new, cache_k, cache_v, cq, cp_t, cn_t, out_buf, *, row0, tp, name):
    streams, n, heads = cq.shape
    hd = cache_k.shape[2]
    past_len = cache_k.shape[1] // heads
    base = row0 // n
    return pl.pallas_call(
        functools.partial(_fox_cached_kernel, heads=heads, tp=tp),
        grid=(streams, past_len // tp),
        in_specs=[pl.BlockSpec((n, heads * hd), lambda s, t: (base + s, 0)),
                  pl.BlockSpec((n, heads * hd), lambda s, t: (base + s, 0)),
                  pl.BlockSpec((n, heads * hd), lambda s, t: (base + s, 0)),
                  pl.BlockSpec((1, tp * heads, hd), lambda s, t: (s, t, 0)),
                  pl.BlockSpec((1, tp * heads, hd), lambda s, t: (s, t, 0)),
                  pl.BlockSpec((1, n, heads), lambda s, t: (s, 0, 0)),
                  pl.BlockSpec((1, heads, 1, tp), lambda s, t: (s, 0, 0, t)),
                  pl.BlockSpec((1, heads, 1, n), lambda s, t: (s, 0, 0, 0)),
                  pl.BlockSpec(memory_space=pl.ANY)],
        out_specs=pl.BlockSpec((n, heads * hd), lambda s, t: (base + s, 0)),
        out_shape=jax.ShapeDtypeStruct(out_buf.shape, BF16),
        scratch_shapes=[pltpu.VMEM((heads, n, 1), F32), pltpu.VMEM((heads, n, 1), F32),
                        pltpu.VMEM((heads, n, hd), F32)],
        input_output_aliases={8: 0},
        compiler_params=_params("parallel", "arbitrary"),
        name=name,
    )(q, k_new, v_new, cache_k, cache_v, cq,
      cp_t.reshape(streams, heads, 1, past_len), cn_t.reshape(streams, heads, 1, n), out_buf)


def kernel(x_prompt, x_sample, state_gla, cache_k, cache_v, cache_logf, g_mix, g_ffn, gla_w_in, gla_w_gk2, gla_b_gk, gla_g_out, gla_w_out, kv_g, kv_w, kv_b_f, kv_g_k, fox_w_q, fox_g_q, fox_w_o, ffn_w_in, ffn_w_out):
    batch, seq, d = x_prompt.shape
    dec_batch, dec_seq, _ = x_sample.shape
    heads = FOX_HEADS
    hd = d // heads
    n_p = batch * seq
    n_s = dec_batch * dec_seq
    assert batch == 1 and g_mix.shape[0] == 2 and state_gla.shape[0] == 1
    assert n_p % ROW_TILE == 0 and n_s == ROW_TILE
    p_tiles = n_p // ROW_TILE
    gla_hk = gla_w_gk2.shape[2]
    gla_hv = gla_w_out.shape[1]
    gla_main = 2 * gla_hk + 2 * gla_hv
    gla_dk, gla_dv = gla_hk // GLA_HEADS, gla_hv // GLA_HEADS
    past_len = cache_k.shape[1]

    x_pair = (x_prompt.reshape(n_p, d), x_sample.reshape(n_s, d))

    w_gla = gla_w_in[0]
    proj = _x_proj(x_pair, g_mix[0], w_gla, n_cols=gla_main, tn=1024, name="gla_proj")
    w_low = jnp.pad(w_gla[:, gla_main:], ((0, 0), (0, LANES - GLA_LOWRANK))).astype(BF16)
    w_gk2 = jnp.pad(gla_w_gk2[0], ((0, LANES - GLA_LOWRANK), (0, 0))).astype(BF16)
    gk = _gla_gate(x_pair, g_mix[0], w_low, w_gk2, gla_b_gk[0], name="gla_gate")
    s0_prompt = jnp.zeros((batch, GLA_HEADS, gla_dk, gla_dv), F32)
    og, state_p = _gla_scan(proj, gk, s0_prompt, gla_g_out[0], row0=0, streams=batch, length=seq,
                            chunk=256, out_buf=None, name="gla_scan_prompt")
    og, state_s = _gla_scan(proj, gk, state_gla[0], gla_g_out[0], row0=n_p, streams=dec_batch,
                            length=dec_seq, chunk=dec_seq, out_buf=og, name="gla_scan_sample")
    h, h16 = _res_matmul(og, gla_w_out[0], x_pair, n_p=p_tiles, tn=1024, out_pair=False, name="gla_out")

    act = _ffn_in(h16, g_ffn[0], ffn_w_in[0], tn=512, name="ffn0_in")
    h, h16 = _res_matmul(act, ffn_w_out[0], h, n_p=p_tiles, tn=512, out_pair=False, name="ffn0_out",
                         vmem=VMEM_LIMIT_WIDE_K)

    k_p, k_s, k16 = _head_proj(h16, kv_g, kv_w, kv_g_k, col0=0, n_p=p_tiles, head_dim=hd, head_norm=True,
                               scale=1.0, emit_f32=True, emit_t=False, tn=1024, name="kv_k")
    v_p, v_s, v16, v16_t = _head_proj(h16, kv_g, kv_w, kv_g_k, col0=d, n_p=p_tiles, head_dim=hd,
                                      head_norm=False, scale=1.0, emit_f32=True, emit_t=True, tn=1024,
                                      name="kv_v")
    (q16,) = _head_proj(h16, g_mix[1], fox_w_q[0], fox_g_q[0], col0=0, n_p=p_tiles, head_dim=hd,
                        head_norm=True, scale=hd ** -0.5 * LOG2E, emit_f32=False, emit_t=False, tn=1024,
                        name="fox_q")
    w_f = jnp.pad(kv_w[:, 2 * d:], ((0, 0), (0, LANES - heads))).astype(BF16)
    b_f = jnp.pad(kv_b_f, (0, LANES - heads))
    logf, logf_t = _logf_proj(h, kv_g, w_f, b_f, name="kv_logf")

    ka, qa = _cumsum_bias(logf, length=n_p, heads=heads, tile=512, name="logf_cumsum_prompt")
    attn = _fox_prompt(q16, k16, v16_t, ka, qa, length=n_p, tile=1024, hp=2, name="fox_attn_prompt")
    new_t = logf_t[:heads, n_p:].reshape(heads, dec_batch, dec_seq).transpose(1, 0, 2)
    cq_s, cp_t, cn_t = _cumsum_cached(cache_logf, cache_logf.transpose(0, 2, 1), logf, new_t,
                                      row0=n_p, name="logf_cumsum_sample")
    attn = _fox_cached(q16, k16, v16, cache_k.reshape(dec_batch, past_len * heads, hd),
                       cache_v.reshape(dec_batch, past_len * heads, hd), cq_s, cp_t, cn_t, attn,
                       row0=n_p, tp=1024, name="fox_attn_sample")
    h, h16 = _res_matmul(attn, fox_w_o[0], h, n_p=p_tiles, tn=1024, out_pair=False, name="fox_out")

    act = _ffn_in(h16, g_ffn[1], ffn_w_in[1], tn=512, name="ffn1_in")
    y_p, y_s = _res_matmul(act, ffn_w_out[1], h, n_p=p_tiles, tn=512, out_pair=True, name="ffn1_out",
                           vmem=VMEM_LIMIT_WIDE_K)

    lf = logf[:, :heads]
    return (y_p.reshape(batch, seq, d), y_s.reshape(dec_batch, dec_seq, d), state_p[None],
            k_p.reshape(batch, seq, heads, hd), v_p.reshape(batch, seq, heads, hd),
            lf[:n_p].reshape(batch, seq, heads), state_s[None],
            k_s.reshape(dec_batch, dec_seq, heads, hd), v_s.reshape(dec_batch, dec_seq, heads, hd),
            lf[n_p:].reshape(dec_batch, dec_seq, heads))
```

```python
import functools
import math

import jax
import jax.numpy as jnp
import numpy as np
from jax import lax
from jax.experimental import pallas as pl
from jax.experimental.pallas import tpu as pltpu

F32 = jnp.float32
BF16 = jnp.bfloat16

EPS = 1e-6
GLA_HEADS = 4
GLA_TAU = 16.0
GLA_LOWRANK = 16
FOX_HEADS = 16
LANES = 128
ROW_TILE = 512
VMEM_LIMIT = 48 * 1024 * 1024
VMEM_LIMIT_WIDE_K = 56 * 1024 * 1024
NEG_BIG = -1e30
QUERY_CHUNK = 256
EXP_ROWS = 128
SUM_ROWS = 16
LOG2E = math.log2(math.e)


def _params(*sem, vmem=VMEM_LIMIT):
    return pltpu.CompilerParams(dimension_semantics=sem, vmem_limit_bytes=vmem)


def _dot(a, b):
    return jnp.dot(a, b, preferred_element_type=F32)


def _dot_nt(a, b):
    return lax.dot_general(a, b, (((1,), (1,)), ((), ())), preferred_element_type=F32)


def _dot_tn(a, b):
    return lax.dot_general(a, b, (((0,), (0,)), ((), ())), preferred_element_type=F32)


def _split3(x):
    hi = x.astype(BF16)
    r1 = x - hi.astype(F32)
    mid = r1.astype(BF16)
    lo = (r1 - mid.astype(F32)).astype(BF16)
    return hi, mid, lo


def _sum3(dot, pieces, other, other_first):
    hi, mid, lo = pieces
    if other_first:
        return (dot(other, lo) + dot(other, mid)) + dot(other, hi)
    return (dot(lo, other) + dot(mid, other)) + dot(hi, other)


def _inv_rms(x):
    return lax.rsqrt(jnp.mean(x * x, axis=-1, keepdims=True) + EPS)


def _rms(x, g):
    return x * _inv_rms(x) * g


def _log_sigmoid(x):
    return jnp.minimum(x, 0.0) - jnp.log1p(jnp.exp(-jnp.abs(x)))


def _silu(x):
    return x / (1.0 + jnp.exp(-x))


def _pair_specs(n_p, cols, col_map):
    return [pl.BlockSpec((ROW_TILE, cols), lambda j, i: (jnp.minimum(i, n_p - 1), col_map(j))),
            pl.BlockSpec((ROW_TILE, cols), lambda j, i: (0, col_map(j)))]


def _pair_load(p_ref, s_ref, n_p):
    return jnp.where(pl.program_id(1) < n_p, p_ref[...], s_ref[...])


def _pair_store(p_ref, s_ref, n_p, val, cols=slice(None)):
    i = pl.program_id(1)

    @pl.when(i < n_p)
    def _():
        p_ref[:, cols] = val

    @pl.when(i >= n_p)
    def _():
        s_ref[:, cols] = val


def _cast_weight(w_scr, w_ref, gain_ref=None):
    @pl.when(pl.program_id(1) == 0)
    def _():
        w = w_ref[...]
        if gain_ref is not None:
            w = w * gain_ref[...]
        w_scr[...] = w.astype(BF16)


def _x_proj_kernel(xp_ref, xs_ref, g_ref, w_ref, o_ref, w_scr, *, n_p):
    _cast_weight(w_scr, w_ref, g_ref)
    x = _pair_load(xp_ref, xs_ref, n_p)
    o_ref[...] = (_inv_rms(x) * _dot(x.astype(BF16), w_scr[...])).astype(o_ref.dtype)


def _x_proj(x_pair, g, w, *, n_cols, tn, name):
    xp, xs = x_pair
    d = xp.shape[1]
    n_p = xp.shape[0] // ROW_TILE
    m = xp.shape[0] + xs.shape[0]
    return pl.pallas_call(
        functools.partial(_x_proj_kernel, n_p=n_p),
        grid=(n_cols // tn, n_p + 1),
        in_specs=_pair_specs(n_p, d, lambda j: 0) + [
            pl.BlockSpec((d, 1), lambda j, i: (0, 0)),
            pl.BlockSpec((d, tn), lambda j, i: (0, j))],
        out_specs=pl.BlockSpec((ROW_TILE, tn), lambda j, i: (i, j)),
        out_shape=jax.ShapeDtypeStruct((m, n_cols), BF16),
        scratch_shapes=[pltpu.VMEM((d, tn), BF16)],
        compiler_params=_params("parallel", "arbitrary"),
        name=name,
    )(xp, xs, g.reshape(d, 1), w)


def _head_proj_kernel(a_ref, g_ref, w_ref, hg_ref, *rest, n_p, head_dim, head_norm, scale, emit_f32, emit_t):
    outs, w_scr = rest[:-1], rest[-1]
    _cast_weight(w_scr, w_ref, g_ref)
    a = a_ref[...]
    af = a.astype(F32)
    acc = _inv_rms(af) * _dot(a, w_scr[...])
    n_f32 = 2 * int(emit_f32)
    ys = []
    for h in range(acc.shape[1] // head_dim):
        y = acc[:, h * head_dim:(h + 1) * head_dim]
        if head_norm:
            y = _rms(y, hg_ref[...])
        if scale != 1.0:
            y = y * scale
        if emit_t:
            outs[n_f32 + 1][h] = y.T.astype(BF16)
        ys.append(y)
    y = jnp.concatenate(ys, axis=1)
    outs[n_f32][...] = y.astype(BF16)
    if emit_f32:
        _pair_store(outs[0], outs[1], n_p, y)


def _head_proj(a, g, w, head_gain, *, col0, n_p, head_dim, head_norm, scale, emit_f32, emit_t, tn, name):
    m, d = a.shape
    n = d
    jb = col0 // tn
    out_spec = pl.BlockSpec((ROW_TILE, tn), lambda j, i: (i, j))
    specs, shapes = [out_spec], [jax.ShapeDtypeStruct((m, n), BF16)]
    if emit_f32:
        specs = _pair_specs(n_p, tn, lambda j: j) + specs
        shapes = [jax.ShapeDtypeStruct((n_p * ROW_TILE, n), F32),
                  jax.ShapeDtypeStruct((m - n_p * ROW_TILE, n), F32)] + shapes
    if emit_t:
        specs.append(pl.BlockSpec((tn // head_dim, head_dim, ROW_TILE), lambda j, i: (j, 0, i)))
        shapes.append(jax.ShapeDtypeStruct((n // head_dim, head_dim, m), BF16))
    return pl.pallas_call(
        functools.partial(_head_proj_kernel, n_p=n_p, head_dim=head_dim, head_norm=head_norm,
                          scale=scale, emit_f32=emit_f32, emit_t=emit_t),
        grid=(n // tn, m // ROW_TILE),
        in_specs=[pl.BlockSpec((ROW_TILE, d), lambda j, i: (i, 0)),
                  pl.BlockSpec((d, 1), lambda j, i: (0, 0)),
                  pl.BlockSpec((d, tn), lambda j, i: (0, jb + j)),
                  pl.BlockSpec((1, head_dim), lambda j, i: (0, 0))],
        out_specs=specs,
        out_shape=shapes,
        scratch_shapes=[pltpu.VMEM((d, tn), BF16)],
        compiler_params=_params("parallel", "arbitrary"),
        name=name,
    )(a, g.reshape(d, 1), w, head_gain.reshape(1, head_dim))


def _ffn_in_kernel(a_ref, g_ref, wg_ref, wu_ref, o_ref, wg_scr, wu_scr):
    _cast_weight(wg_scr, wg_ref, g_ref)
    _cast_weight(wu_scr, wu_ref, g_ref)
    a = a_ref[...]
    af = a.astype(F32)
    r = _inv_rms(af)
    gate = r * _dot(a, wg_scr[...])
    up = r * _dot(a, wu_scr[...])
    o_ref[...] = (_silu(gate) * up).astype(o_ref.dtype)


def _ffn_in(a, g, w_in, *, layer, tn, name):
    m, d = a.shape
    ff = w_in.shape[2] // 2
    nj = ff // tn
    return pl.pallas_call(
        _ffn_in_kernel,
        grid=(nj, m // ROW_TILE),
        in_specs=[pl.BlockSpec((ROW_TILE, d), lambda j, i: (i, 0)),
                  pl.BlockSpec((d, 1), lambda j, i: (0, 0)),
                  pl.BlockSpec((None, d, tn), lambda j, i: (layer, 0, j)),
                  pl.BlockSpec((None, d, tn), lambda j, i: (layer, 0, j + nj))],
        out_specs=pl.BlockSpec((ROW_TILE, tn), lambda j, i: (i, j)),
        out_shape=jax.ShapeDtypeStruct((m, ff), BF16),
        scratch_shapes=[pltpu.VMEM((d, tn), BF16), pltpu.VMEM((d, tn), BF16)],
        compiler_params=_params("parallel", "arbitrary"),
        name=name,
    )(a, g.reshape(d, 1), w_in, w_in)


def _res_kernel(a_ref, w_ref, *rest, n_p, res_pair, out_pair):
    w_scr = rest[-1]
    n_res = 2 if res_pair else 1
    res_refs, out_refs = rest[:n_res], rest[n_res:-1]
    _cast_weight(w_scr, w_ref)
    res = _pair_load(res_refs[0], res_refs[1], n_p) if res_pair else res_refs[0][...]
    y = res + _dot(a_ref[...], w_scr[...])
    if out_pair:
        _pair_store(out_refs[0], out_refs[1], n_p, y)
    else:
        out_refs[0][...] = y
        out_refs[1][...] = y.astype(BF16)


def _res_matmul(a, w, res, *, layer, n_p, tn, out_pair, name, vmem=VMEM_LIMIT):
    m, k = a.shape
    n = w.shape[2]
    res_pair = isinstance(res, tuple)
    tile_spec = pl.BlockSpec((ROW_TILE, tn), lambda j, i: (i, j))
    res_specs = _pair_specs(n_p, tn, lambda j: j) if res_pair else [tile_spec]
    if out_pair:
        out_specs = _pair_specs(n_p, tn, lambda j: j)
        out_shape = [jax.ShapeDtypeStruct((n_p * ROW_TILE, n), F32),
                     jax.ShapeDtypeStruct((m - n_p * ROW_TILE, n), F32)]
    else:
        out_specs = [tile_spec, tile_spec]
        out_shape = [jax.ShapeDtypeStruct((m, n), F32), jax.ShapeDtypeStruct((m, n), BF16)]
    return pl.pallas_call(
        functools.partial(_res_kernel, n_p=n_p, res_pair=res_pair, out_pair=out_pair),
        grid=(n // tn, m // ROW_TILE),
        in_specs=[pl.BlockSpec((ROW_TILE, k), lambda j, i: (i, 0)),
                  pl.BlockSpec((None, k, tn), lambda j, i: (layer, 0, j))] + res_specs,
        out_specs=out_specs,
        out_shape=out_shape,
        scratch_shapes=[pltpu.VMEM((k, tn), BF16)],
        compiler_params=_params("parallel", "arbitrary", vmem=vmem),
        name=name,
    )(a, w, *(res if res_pair else (res,)))


def _gla_gate_kernel(xp_ref, xs_ref, g_ref, wl_ref, w2_ref, b_ref, o_ref, *, n_p):
    x = jnp.where(pl.program_id(0) < n_p, xp_ref[...], xs_ref[...])
    u = _rms(x, g_ref[...]).astype(BF16)
    low = _dot(u, wl_ref[...]).astype(BF16)
    o_ref[...] = _log_sigmoid(_dot(low, w2_ref[...]) + b_ref[...]) * (1.0 / GLA_TAU)


def _gla_gate(x_pair, g, w_low, w_gk2, b_gk, *, name):
    xp, xs = x_pair
    d = xp.shape[1]
    n_p = xp.shape[0] // ROW_TILE
    m = xp.shape[0] + xs.shape[0]
    n = w_gk2.shape[1]
    return pl.pallas_call(
        functools.partial(_gla_gate_kernel, n_p=n_p),
        grid=(n_p + 1,),
        in_specs=[pl.BlockSpec((ROW_TILE, d), lambda i: (jnp.minimum(i, n_p - 1), 0)),
                  pl.BlockSpec((ROW_TILE, d), lambda i: (0, 0)),
                  pl.BlockSpec((1, d), lambda i: (0, 0)),
                  pl.BlockSpec((d, LANES), lambda i: (0, 0)),
                  pl.BlockSpec((LANES, n), lambda i: (0, 0)),
                  pl.BlockSpec((1, n), lambda i: (0, 0))],
        out_specs=pl.BlockSpec((ROW_TILE, n), lambda i: (i, 0)),
        out_shape=jax.ShapeDtypeStruct((m, n), F32),
        compiler_params=_params("arbitrary"),
        name=name,
    )(xp, xs, g.reshape(1, d), w_low, w_gk2, b_gk.reshape(1, n))


def _logf_kernel(x_ref, g_ref, w_ref, wt_ref, b_ref, bt_ref, o_ref, ot_ref):
    u = _rms(x_ref[...], g_ref[...]).astype(BF16)
    o_ref[...] = _log_sigmoid(_dot(u, w_ref[...]) + b_ref[...])
    ot_ref[...] = _log_sigmoid(_dot_nt(wt_ref[...], u) + bt_ref[...])


def _logf_proj(x, g, w_f, b_f, *, name):
    m, d = x.shape
    return pl.pallas_call(
        _logf_kernel,
        grid=(m // ROW_TILE,),
        in_specs=[pl.BlockSpec((ROW_TILE, d), lambda i: (i, 0)),
                  pl.BlockSpec((1, d), lambda i: (0, 0)),
                  pl.BlockSpec((d, LANES), lambda i: (0, 0)),
                  pl.BlockSpec((LANES, d), lambda i: (0, 0)),
                  pl.BlockSpec((1, LANES), lambda i: (0, 0)),
                  pl.BlockSpec((LANES, 1), lambda i: (0, 0))],
        out_specs=[pl.BlockSpec((ROW_TILE, LANES), lambda i: (i, 0)),
                   pl.BlockSpec((LANES, ROW_TILE), lambda i: (0, i))],
        out_shape=[jax.ShapeDtypeStruct((m, LANES), F32),
                   jax.ShapeDtypeStruct((LANES, m), F32)],
        compiler_params=_params("parallel"),
        name=name,
    )(x, g.reshape(1, d), w_f, w_f.T, b_f.reshape(1, LANES), b_f.reshape(LANES, 1))


def _block_ref_rows(b, h):
    c, d = b.shape
    span = 2 * h
    if span >= 8:
        b3 = b.reshape(c // span, span, d)
        return jnp.broadcast_to(b3[:, h:h + 1, :], b3.shape).reshape(c, d)
    b3 = b.reshape(c // 8, 8, d)
    sub = lax.broadcasted_iota(jnp.int32, b3.shape, 1)
    ref = jnp.broadcast_to(b3[:, h:h + 1, :], b3.shape)
    for blk in range(1, 8 // span):
        cand = jnp.broadcast_to(b3[:, blk * span + h:blk * span + h + 1, :], b3.shape)
        ref = jnp.where(sub >= blk * span, cand, ref)
    return ref.reshape(c, d)


def _gla_kernel(q_ref, k_ref, v_ref, gate_ref, gk_ref, s0_ref, gout_ref, *rest, chunk):
    o_ref, sn_ref, s_scr = rest[-3:]
    c_idx = pl.program_id(2)

    @pl.when(c_idx == 0)
    def _():
        s_scr[...] = s0_ref[0, 0]

    dk = q_ref.shape[1]
    q = q_ref[...].astype(F32) * (dk ** -0.5)
    k = k_ref[...].astype(F32)
    v = v_ref[...]
    row = lax.broadcasted_iota(jnp.int32, (chunk, chunk), 0)
    col = lax.broadcasted_iota(jnp.int32, (chunk, chunk), 1)
    tri = jnp.where(row >= col, 1.0, 0.0).astype(BF16)
    g3 = _split3(gk_ref[...])
    b = _sum3(_dot, g3, tri, other_first=True)
    b_end = b[chunk - 1:chunk, :]
    b_end_col = _sum3(_dot_tn, g3, jnp.ones((chunk, LANES), BF16), other_first=False)

    s_prev = s_scr[...]
    o = _dot((q * jnp.exp(b)).astype(BF16), s_prev.astype(BF16))

    att = jnp.where(row == col, _dot_nt(q.astype(BF16), k.astype(BF16)), 0.0)
    split_bit = jnp.where(row > col, row ^ col, 0)
    sub_row = lax.broadcasted_iota(jnp.int32, (chunk, dk), 0)
    h = 1
    while h < chunk:
        f = jnp.exp(-jnp.abs(b - _block_ref_rows(b, h)))
        z = (jnp.where((sub_row & h) != 0, q, k) * f).astype(BF16)
        att = jnp.where(split_bit >= h, _dot_nt(z, z), att)
        h *= 2
    o = o + _dot(att.astype(BF16), v)

    k_dec = (k * jnp.exp(b_end - b)).astype(BF16)
    decay = jnp.exp(b_end_col)
    decay = jnp.concatenate([decay] * (s_prev.shape[1] // LANES), axis=1)
    s_new = decay * s_prev + _dot_tn(k_dec, v)
    s_scr[...] = s_new

    @pl.when(c_idx == pl.num_programs(2) - 1)
    def _():
        sn_ref[0, 0] = s_new

    o_ref[...] = (_rms(o, gout_ref[...]) * _silu(gate_ref[...].astype(F32))).astype(o_ref.dtype)


def _gla_scan(proj, gk, s0, g_out, *, row0, streams, length, chunk, out_buf, name):
    heads = GLA_HEADS
    dk = gk.shape[1] // heads
    dv = s0.shape[-1]
    n_chunks = length // chunk
    base = row0 // chunk
    kq, vq = dk * heads // dk, 2 * dk * heads // dv

    def rows(s, h, c):
        return base + s * n_chunks + c

    in_specs = [pl.BlockSpec((chunk, dk), lambda s, h, c: (rows(s, h, c), h)),
                pl.BlockSpec((chunk, dk), lambda s, h, c: (rows(s, h, c), kq + h)),
                pl.BlockSpec((chunk, dv), lambda s, h, c: (rows(s, h, c), vq + h)),
                pl.BlockSpec((chunk, dv), lambda s, h, c: (rows(s, h, c), vq + heads + h)),
                pl.BlockSpec((chunk, dk), lambda s, h, c: (rows(s, h, c), h)),
                pl.BlockSpec((1, 1, dk, dv), lambda s, h, c: (s, h, 0, 0)),
                pl.BlockSpec((1, dv), lambda s, h, c: (0, 0))]
    args = [proj, proj, proj, proj, gk, s0, g_out.reshape(1, dv)]
    aliases = {}
    if out_buf is not None:
        in_specs.append(pl.BlockSpec(memory_space=pl.ANY))
        args.append(out_buf)
        aliases = {len(args) - 1: 0}
    return pl.pallas_call(
        functools.partial(_gla_kernel, chunk=chunk),
        grid=(streams, heads, n_chunks),
        in_specs=in_specs,
        out_specs=[pl.BlockSpec((chunk, dv), lambda s, h, c: (rows(s, h, c), h)),
                   pl.BlockSpec((1, 1, dk, dv), lambda s, h, c: (s, h, 0, 0))],
        out_shape=[jax.ShapeDtypeStruct((proj.shape[0], heads * dv), BF16),
                   jax.ShapeDtypeStruct((streams, heads, dk, dv), F32)],
        scratch_shapes=[pltpu.VMEM((dk, dv), F32)],
        input_output_aliases=aliases,
        compiler_params=_params("parallel", "parallel", "arbitrary"),
        name=name,
    )(*args)


def _bias_lane_maps(heads):
    maps = np.zeros((heads, 3 * LANES, 2 * LANES), np.float32)
    for h in range(heads):
        for piece in range(3):
            maps[h, piece * LANES + h, 3 + piece] = -1.0
            maps[h, piece * LANES + h, LANES + piece] = 1.0
    const = np.zeros((1, 2 * LANES), np.float32)
    const[0, 0:3] = 1.0
    const[0, LANES + 3:LANES + 6] = 1.0
    return jnp.asarray(maps, BF16), jnp.asarray(const)


def _cumsum_bias_kernel(x_ref, maps_ref, const_ref, ka_ref, qa_ref, carry_scr, *, tile, heads):
    @pl.when(pl.program_id(0) == 0)
    def _():
        carry_scr[...] = jnp.zeros_like(carry_scr)

    r = lax.broadcasted_iota(jnp.int32, (tile, tile), 0)
    c = lax.broadcasted_iota(jnp.int32, (tile, tile), 1)
    x = x_ref[...]
    lower = jnp.where(r >= c, 1.0, 0.0).astype(BF16)
    cum = carry_scr[...] + _sum3(_dot, _split3(x), lower, other_first=True)
    carry_scr[...] += jnp.sum(x, axis=0, keepdims=True)
    pieces = jnp.concatenate(_split3(cum * LOG2E), axis=1)
    for h in range(heads):
        lanes = (_dot(pieces, maps_ref[h]) + const_ref[...]).astype(BF16)
        ka_ref[h] = lanes[:, :LANES]
        qa_ref[h] = lanes[:, LANES:]


def _cumsum_bias(x, *, length, heads, tile, name):
    maps, const = _bias_lane_maps(heads)
    out = jax.ShapeDtypeStruct((heads, length, LANES), BF16)
    return pl.pallas_call(
        functools.partial(_cumsum_bias_kernel, tile=tile, heads=heads),
        grid=(length // tile,),
        in_specs=[pl.BlockSpec((tile, LANES), lambda i: (i, 0)),
                  pl.BlockSpec(maps.shape, lambda i: (0, 0, 0)),
                  pl.BlockSpec(const.shape, lambda i: (0, 0))],
        out_specs=[pl.BlockSpec((heads, tile, LANES), lambda i: (0, i, 0)),
                   pl.BlockSpec((heads, tile, LANES), lambda i: (0, i, 0))],
        out_shape=[out, out],
        scratch_shapes=[pltpu.VMEM((1, LANES), F32)],
        compiler_params=_params("arbitrary"),
        name=name,
    )(x, maps, const)


def _cumsum_cached_kernel(past_ref, pastt_ref, new_ref, newt_ref, cq_ref, cpt_ref, cnt_ref, *, tile):
    past_t = pastt_ref[0]
    past_len = past_t.shape[1]
    r = lax.broadcasted_iota(jnp.int32, (tile, tile), 0)
    c = lax.broadcasted_iota(jnp.int32, (tile, tile), 1)
    upper = jnp.where(r <= c, 1.0, 0.0).astype(BF16)
    carry = jnp.zeros((past_t.shape[0], 1), F32)
    for t in range(past_len // tile):
        xt = past_t[:, t * tile:(t + 1) * tile]
        cpt_ref[0, :, t * tile:(t + 1) * tile] = carry + _sum3(_dot, _split3(xt), upper, other_first=False)
        carry = carry + jnp.sum(xt, axis=1, keepdims=True)
    n = newt_ref.shape[2]
    rn = lax.broadcasted_iota(jnp.int32, (n, n), 0)
    cn = lax.broadcasted_iota(jnp.int32, (n, n), 1)
    cnt_ref[0] = carry + _sum3(_dot, _split3(newt_ref[0]),
                               jnp.where(rn <= cn, 1.0, 0.0).astype(BF16), other_first=False)
    heads = past_t.shape[0]
    past_total = jnp.sum(past_ref[0], axis=0, keepdims=True)
    c_new = _sum3(_dot, _split3(new_ref[...]), jnp.where(rn >= cn, 1.0, 0.0).astype(BF16), other_first=True)
    cq_ref[0] = past_total + c_new[:, :heads]


def _cumsum_cached(past, past_t, new, new_t, *, row0, name):
    streams, past_len, heads = past.shape
    n = new_t.shape[2]
    base = row0 // n
    return pl.pallas_call(
        functools.partial(_cumsum_cached_kernel, tile=512),
        grid=(streams,),
        in_specs=[pl.BlockSpec((1, past_len, heads), lambda s: (s, 0, 0)),
                  pl.BlockSpec((1, heads, past_len), lambda s: (s, 0, 0)),
                  pl.BlockSpec((n, LANES), lambda s: (base + s, 0)),
                  pl.BlockSpec((1, heads, n), lambda s: (s, 0, 0))],
        out_specs=[pl.BlockSpec((1, n, heads), lambda s: (s, 0, 0)),
                   pl.BlockSpec((1, heads, past_len), lambda s: (s, 0, 0)),
                   pl.BlockSpec((1, heads, n), lambda s: (s, 0, 0))],
        out_shape=[jax.ShapeDtypeStruct((streams, n, heads), F32),
                   jax.ShapeDtypeStruct((streams, heads, past_len), F32),
                   jax.ShapeDtypeStruct((streams, heads, n), F32)],
        compiler_params=_params("parallel"),
        name=name,
    )(past, past_t, new, new_t)


def _fox_prompt_kernel(qi_ref, kj_ref, q_ref, qa_ref, k_ref, ka_ref, vt_ref, o_ref,
                       m_scr, acc_scr, p_scr, *, hp):
    step = pl.program_id(1)
    qi = qi_ref[step]
    kj = kj_ref[step]
    tq, tk = q_ref.shape[0], k_ref.shape[0]
    hd = q_ref.shape[1] // hp

    @pl.when(kj == 0)
    def _():
        m_scr[...] = jnp.full_like(m_scr, NEG_BIG)
        acc_scr[...] = jnp.zeros_like(acc_scr)

    units = [(h, c) for h in range(hp) for c in range(tq // QUERY_CHUNK)]

    def scores(unit, masked):
        h, c = unit
        sl = slice(h * hd, (h + 1) * hd)
        rows = slice(c * QUERY_CHUNK, (c + 1) * QUERY_CHUNK)
        keys = (c + 1) * QUERY_CHUNK if masked else tk
        q_aug = jnp.concatenate([q_ref[rows, sl], qa_ref[h, rows, :]], axis=1)
        k_aug = jnp.concatenate([k_ref[:keys, sl], ka_ref[h, :keys, :]], axis=1)
        s = _dot_nt(k_aug, q_aug)
        if masked:
            key = lax.broadcasted_iota(jnp.int32, s.shape, 0)
            qry = lax.broadcasted_iota(jnp.int32, s.shape, 1) + c * QUERY_CHUNK
            s = jnp.where(key <= qry, s, NEG_BIG)
        return s

    def update(masked):
        s_next = scores(units[0], masked)
        for i, (h, c) in enumerate(units):
            s = s_next
            if i + 1 < len(units):
                s_next = scores(units[i + 1], masked)
            keys = s.shape[0]
            cols = slice(c * QUERY_CHUNK, (c + 1) * QUERY_CHUNK)
            m_prev = m_scr[h, :, cols]
            m_new = jnp.maximum(m_prev, jnp.max(s, axis=0, keepdims=True))
            alpha = jnp.exp2(m_prev - m_new)
            m_rows = jnp.broadcast_to(m_new, (8, QUERY_CHUNK))
            for r0 in range(0, keys, EXP_ROWS):
                sc = s[r0:r0 + EXP_ROWS, :].reshape(EXP_ROWS // 8, 8, QUERY_CHUNK)
                p_scr[i % 2, r0:r0 + EXP_ROWS, :] = (
                    jnp.exp2(sc - m_rows).reshape(EXP_ROWS, QUERY_CHUNK).astype(BF16))
            vt_aug = jnp.concatenate([vt_ref[h, :, :keys], jnp.ones((SUM_ROWS, keys), BF16)], axis=0)
            acc_scr[h, :, cols] = alpha * acc_scr[h, :, cols] + _dot(vt_aug, p_scr[i % 2, :keys, :])
            m_scr[h, :, cols] = m_new

    @pl.when(kj < qi)
    def _():
        update(False)

    @pl.when(kj == qi)
    def _():
        update(True)
        for h in range(hp):
            acc = acc_scr[h]
            o_ref[:, h * hd:(h + 1) * hd] = (acc[:hd] / acc[hd:hd + 1]).T.astype(o_ref.dtype)


def _fox_prompt(q, k, v_t, ka, qa, *, length, tile, hp, name):
    heads, hd, _ = v_t.shape
    nq = length // tile
    pairs = [(i, j) for i in range(nq) for j in range(i + 1)]
    qi = jnp.asarray(np.array([p[0] for p in pairs], np.int32))
    kj = jnp.asarray(np.array([p[1] for p in pairs], np.int32))
    grid_spec = pltpu.PrefetchScalarGridSpec(
        num_scalar_prefetch=2,
        grid=(heads // hp, len(pairs)),
        in_specs=[pl.BlockSpec((tile, hp * hd), lambda g, s, qi, kj: (qi[s], g)),
                  pl.BlockSpec((hp, tile, LANES), lambda g, s, qi, kj: (g, qi[s], 0)),
                  pl.BlockSpec((tile, hp * hd), lambda g, s, qi, kj: (kj[s], g)),
                  pl.BlockSpec((hp, tile, LANES), lambda g, s, qi, kj: (g, kj[s], 0)),
                  pl.BlockSpec((hp, hd, tile), lambda g, s, qi, kj: (g, 0, kj[s]))],
        out_specs=pl.BlockSpec((tile, hp * hd), lambda g, s, qi, kj: (qi[s], g)),
        scratch_shapes=[pltpu.VMEM((hp, 1, tile), F32),
                        pltpu.VMEM((hp, hd + SUM_ROWS, tile), F32),
                        pltpu.VMEM((2, tile, QUERY_CHUNK), BF16)],
    )
    return pl.pallas_call(
        functools.partial(_fox_prompt_kernel, hp=hp),
        grid_spec=grid_spec,
        out_shape=jax.ShapeDtypeStruct((q.shape[0], heads * hd), BF16),
        compiler_params=_params("parallel", "arbitrary"),
        name=name,
    )(qi, kj, q, qa, k, ka, v_t)


def _fox_cached_kernel(q_ref, kn_ref, vn_ref, kp_ref, vp_ref, cq_ref, cp_ref, cn_ref, buf_ref, o_ref,
                       m_scr, l_scr, acc_scr, *, heads, tp):
    t = pl.program_id(1)
    n = q_ref.shape[0]
    hd = q_ref.shape[1] // heads

    @pl.when(t == 0)
    def _():
        m_scr[...] = jnp.full_like(m_scr, NEG_BIG)
        l_scr[...] = jnp.zeros_like(l_scr)
        acc_scr[...] = jnp.zeros_like(acc_scr)

    def update(h, s, v):
        m_prev = m_scr[h]
        m_new = jnp.maximum(m_prev, jnp.max(s, axis=1, keepdims=True))
        alpha = jnp.exp2(m_prev - m_new)
        p = jnp.exp2(s - m_new)
        l_scr[h] = alpha * l_scr[h] + jnp.sum(p, axis=1, keepdims=True)
        acc_scr[h] = alpha * acc_scr[h] + _dot(p.astype(BF16), v)
        m_scr[h] = m_new

    for h in range(heads):
        sl = slice(h * hd, (h + 1) * hd)
        k = kp_ref[0, pl.ds(h, tp, stride=heads), :].astype(BF16)
        v = vp_ref[0, pl.ds(h, tp, stride=heads), :].astype(BF16)
        bias = (cq_ref[0][:, h:h + 1] - cp_ref[0, h]) * LOG2E
        update(h, _dot_nt(q_ref[:, sl], k) + bias, v)

    @pl.when(t == pl.num_programs(1) - 1)
    def _():
        row = lax.broadcasted_iota(jnp.int32, (n, n), 0)
        col = lax.broadcasted_iota(jnp.int32, (n, n), 1)
        for h in range(heads):
            sl = slice(h * hd, (h + 1) * hd)
            bias = (cq_ref[0][:, h:h + 1] - cn_ref[0, h]) * LOG2E
            s = jnp.where(col <= row, _dot_nt(q_ref[:, sl], kn_ref[:, sl]) + bias, NEG_BIG)
            update(h, s, vn_ref[:, sl])
            o_ref[:, sl] = (acc_scr[h] / l_scr[h]).astype(o_ref.dtype)


def _fox_cached(q, k_new, v_new, cache_k, cache_v, cq, cp_t, cn_t, out_buf, *, row0, tp, name):
    streams, n, heads = cq.shape
    hd = cache_k.shape[2]
    past_len = cache_k.shape[1] // heads
    base = row0 // n
    return pl.pallas_call(
        functools.partial(_fox_cached_kernel, heads=heads, tp=tp),
        grid=(streams, past_len // tp),
        in_specs=[pl.BlockSpec((n, heads * hd), lambda s, t: (base + s, 0)),
                  pl.BlockSpec((n, heads * hd), lambda s, t: (base + s, 0)),
                  pl.BlockSpec((n, heads * hd), lambda s, t: (base + s, 0)),
                  pl.BlockSpec((1, tp * heads, hd), lambda s, t: (s, t, 0)),
                  pl.BlockSpec((1, tp * heads, hd), lambda s, t: (s, t, 0)),
                  pl.BlockSpec((1, n, heads), lambda s, t: (s, 0, 0)),
                  pl.BlockSpec((1, heads, 1, tp), lambda s, t: (s, 0, 0, t)),
                  pl.BlockSpec((1, heads, 1, n), lambda s, t: (s, 0, 0, 0)),
                  pl.BlockSpec(memory_space=pl.ANY)],
        out_specs=pl.BlockSpec((n, heads * hd), lambda s, t: (base + s, 0)),
        out_shape=jax.ShapeDtypeStruct(out_buf.shape, BF16),
        scratch_shapes=[pltpu.VMEM((heads, n, 1), F32), pltpu.VMEM((heads, n, 1), F32),
                        pltpu.VMEM((heads, n, hd), F32)],
        input_output_aliases={8: 0},
        compiler_params=_params("parallel", "arbitrary"),
        name=name,
    )(q, k_new, v_new, cache_k, cache_v, cq,
      cp_t.reshape(streams, heads, 1, past_len), cn_t.reshape(streams, heads, 1, n), out_buf)


def kernel(x_prompt, x_sample, state_gla, cache_k, cache_v, cache_logf, g_mix, g_ffn, gla_w_in, gla_w_gk2, gla_b_gk, gla_g_out, gla_w_out, kv_g, kv_w, kv_b_f, kv_g_k, fox_w_q, fox_g_q, fox_w_o, ffn_w_in, ffn_w_out):
    batch, seq, d = x_prompt.shape
    dec_batch, dec_seq, _ = x_sample.shape
    heads = FOX_HEADS
    hd = d // heads
    n_p = batch * seq
    n_s = dec_batch * dec_seq
    assert batch == 1 and g_mix.shape[0] == 2 and state_gla.shape[0] == 1
    assert n_p % ROW_TILE == 0 and n_s == ROW_TILE
    p_tiles = n_p // ROW_TILE
    gla_hk = gla_w_gk2.shape[2]
    gla_hv = gla_w_out.shape[1]
    gla_main = 2 * gla_hk + 2 * gla_hv
    gla_dk, gla_dv = gla_hk // GLA_HEADS, gla_hv // GLA_HEADS
    past_len = cache_k.shape[1]

    x_pair = (x_prompt.reshape(n_p, d), x_sample.reshape(n_s, d))

    w_gla = gla_w_in[0]
    proj = _x_proj(x_pair, g_mix[0], w_gla, n_cols=gla_main, tn=1024, name="gla_proj")
    w_low = jnp.pad(w_gla[:, gla_main:], ((0, 0), (0, LANES - GLA_LOWRANK))).astype(BF16)
    w_gk2 = jnp.pad(gla_w_gk2[0], ((0, LANES - GLA_LOWRANK), (0, 0))).astype(BF16)
    gk = _gla_gate(x_pair, g_mix[0], w_low, w_gk2, gla_b_gk[0], name="gla_gate")
    s0_prompt = jnp.zeros((batch, GLA_HEADS, gla_dk, gla_dv), F32)
    og, state_p = _gla_scan(proj, gk, s0_prompt, gla_g_out[0], row0=0, streams=batch, length=seq,
                            chunk=256, out_buf=None, name="gla_scan_prompt")
    og, state_s = _gla_scan(proj, gk, state_gla[0], gla_g_out[0], row0=n_p, streams=dec_batch,
                            length=dec_seq, chunk=dec_seq, out_buf=og, name="gla_scan_sample")
    h, h16 = _res_matmul(og, gla_w_out, x_pair, layer=0, n_p=p_tiles, tn=1024, out_pair=False,
                         name="gla_out")

    act = _ffn_in(h16, g_ffn[0], ffn_w_in, layer=0, tn=512, name="ffn0_in")
    h, h16 = _res_matmul(act, ffn_w_out, h, layer=0, n_p=p_tiles, tn=512, out_pair=False, name="ffn0_out",
                         vmem=VMEM_LIMIT_WIDE_K)

    k_p, k_s, k16 = _head_proj(h16, kv_g, kv_w, kv_g_k, col0=0, n_p=p_tiles, head_dim=hd, head_norm=True,
                               scale=1.0, emit_f32=True, emit_t=False, tn=1024, name="kv_k")
    v_p, v_s, v16, v16_t = _head_proj(h16, kv_g, kv_w, kv_g_k, col0=d, n_p=p_tiles, head_dim=hd,
                                      head_norm=False, scale=1.0, emit_f32=True, emit_t=True, tn=1024,
                                      name="kv_v")
    (q16,) = _head_proj(h16, g_mix[1], fox_w_q[0], fox_g_q[0], col0=0, n_p=p_tiles, head_dim=hd,
                        head_norm=True, scale=hd ** -0.5 * LOG2E, emit_f32=False, emit_t=False, tn=1024,
                        name="fox_q")
    w_f = jnp.pad(kv_w[:, 2 * d:], ((0, 0), (0, LANES - heads))).astype(BF16)
    b_f = jnp.pad(kv_b_f, (0, LANES - heads))
    logf, logf_t = _logf_proj(h, kv_g, w_f, b_f, name="kv_logf")

    ka, qa = _cumsum_bias(logf, length=n_p, heads=heads, tile=512, name="logf_cumsum_prompt")
    attn = _fox_prompt(q16, k16, v16_t, ka, qa, length=n_p, tile=1024, hp=2, name="fox_attn_prompt")
    new_t = logf_t[:heads, n_p:].reshape(heads, dec_batch, dec_seq).transpose(1, 0, 2)
    cq_s, cp_t, cn_t = _cumsum_cached(cache_logf, cache_logf.transpose(0, 2, 1), logf, new_t,
                                      row0=n_p, name="logf_cumsum_sample")
    attn = _fox_cached(q16, k16, v16, cache_k.reshape(dec_batch, past_len * heads, hd),
                       cache_v.reshape(dec_batch, past_len * heads, hd), cq_s, cp_t, cn_t, attn,
                       row0=n_p, tp=1024, name="fox_attn_sample")
    h, h16 = _res_matmul(attn, fox_w_o, h, layer=0, n_p=p_tiles, tn=1024, out_pair=False, name="fox_out")

    act = _ffn_in(h16, g_ffn[1], ffn_w_in, layer=1, tn=512, name="ffn1_in")
    y_p, y_s = _res_matmul(act, ffn_w_out, h, layer=1, n_p=p_tiles, tn=512, out_pair=True, name="ffn1_out",
                           vmem=VMEM_LIMIT_WIDE_K)

    lf = logf[:, :heads]
    return (y_p.reshape(batch, seq, d), y_s.reshape(dec_batch, dec_seq, d), state_p[None],
            k_p.reshape(batch, seq, heads, hd), v_p.reshape(batch, seq, heads, hd),
            lf[:n_p].reshape(batch, seq, heads), state_s[None],
            k_s.reshape(dec_batch, dec_seq, heads, hd), v_s.reshape(dec_batch, dec_seq, heads, hd),
            lf[n_p:].reshape(dec_batch, dec_seq, heads))
```

```python
import functools
import math

import jax
import jax.numpy as jnp
import numpy as np
from jax import lax
from jax.experimental import pallas as pl
from jax.experimental.pallas import tpu as pltpu

F32 = jnp.float32
BF16 = jnp.bfloat16

EPS = 1e-6
GLA_HEADS = 4
GLA_TAU = 16.0
GLA_LOWRANK = 16
FOX_HEADS = 16
LANES = 128
ROW_TILE = 512
VMEM_LIMIT = 48 * 1024 * 1024
VMEM_LIMIT_WIDE_K = 56 * 1024 * 1024
NEG_BIG = -1e30
QUERY_CHUNK = 256
EXP_ROWS = 128
SUM_ROWS = 16
LOG2E = math.log2(math.e)


def _params(*sem, vmem=VMEM_LIMIT):
    return pltpu.CompilerParams(dimension_semantics=sem, vmem_limit_bytes=vmem)


def _dot(a, b):
    return jnp.dot(a, b, preferred_element_type=F32)


def _dot_nt(a, b):
    return lax.dot_general(a, b, (((1,), (1,)), ((), ())), preferred_element_type=F32)


def _dot_tn(a, b):
    return lax.dot_general(a, b, (((0,), (0,)), ((), ())), preferred_element_type=F32)


def _split3(x):
    hi = x.astype(BF16)
    r1 = x - hi.astype(F32)
    mid = r1.astype(BF16)
    lo = (r1 - mid.astype(F32)).astype(BF16)
    return hi, mid, lo


def _sum3(dot, pieces, other, other_first):
    hi, mid, lo = pieces
    if other_first:
        return (dot(other, lo) + dot(other, mid)) + dot(other, hi)
    return (dot(lo, other) + dot(mid, other)) + dot(hi, other)


def _inv_rms(x):
    return lax.rsqrt(jnp.mean(x * x, axis=-1, keepdims=True) + EPS)


def _rms(x, g):
    return x * _inv_rms(x) * g


def _log_sigmoid(x):
    return jnp.minimum(x, 0.0) - jnp.log1p(jnp.exp(-jnp.abs(x)))


def _silu(x):
    return x / (1.0 + jnp.exp(-x))


def _pair_specs(n_p, cols, col_map):
    return [pl.BlockSpec((ROW_TILE, cols), lambda j, i: (jnp.minimum(i, n_p - 1), col_map(j))),
            pl.BlockSpec((ROW_TILE, cols), lambda j, i: (0, col_map(j)))]


def _pair_load(p_ref, s_ref, n_p):
    return jnp.where(pl.program_id(1) < n_p, p_ref[...], s_ref[...])


def _pair_store(p_ref, s_ref, n_p, val, cols=slice(None)):
    i = pl.program_id(1)

    @pl.when(i < n_p)
    def _():
        p_ref[:, cols] = val

    @pl.when(i >= n_p)
    def _():
        s_ref[:, cols] = val


def _cast_weight(w_scr, w_ref, gain_ref=None):
    @pl.when(pl.program_id(1) == 0)
    def _():
        w = w_ref[...]
        if gain_ref is not None:
            w = w * gain_ref[...]
        w_scr[...] = w.astype(BF16)


def _x_proj_kernel(xp_ref, xs_ref, g_ref, w_ref, o_ref, w_scr, *, n_p):
    _cast_weight(w_scr, w_ref, g_ref)
    x = _pair_load(xp_ref, xs_ref, n_p)
    o_ref[...] = (_inv_rms(x) * _dot(x.astype(BF16), w_scr[...])).astype(o_ref.dtype)


def _x_proj(x_pair, g, w, *, n_cols, tn, name):
    xp, xs = x_pair
    d = xp.shape[1]
    n_p = xp.shape[0] // ROW_TILE
    m = xp.shape[0] + xs.shape[0]
    return pl.pallas_call(
        functools.partial(_x_proj_kernel, n_p=n_p),
        grid=(n_cols // tn, n_p + 1),
        in_specs=_pair_specs(n_p, d, lambda j: 0) + [
            pl.BlockSpec((d, 1), lambda j, i: (0, 0)),
            pl.BlockSpec((d, tn), lambda j, i: (0, j))],
        out_specs=pl.BlockSpec((ROW_TILE, tn), lambda j, i: (i, j)),
        out_shape=jax.ShapeDtypeStruct((m, n_cols), BF16),
        scratch_shapes=[pltpu.VMEM((d, tn), BF16)],
        compiler_params=_params("parallel", "arbitrary"),
        name=name,
    )(xp, xs, g.reshape(d, 1), w)


def _head_proj_kernel(a_ref, g_ref, w_ref, hg_ref, *rest, n_p, head_dim, head_norm, scale, emit_f32, emit_t):
    outs, w_scr = rest[:-1], rest[-1]
    _cast_weight(w_scr, w_ref, g_ref)
    a = a_ref[...]
    af = a.astype(F32)
    acc = _inv_rms(af) * _dot(a, w_scr[...])
    n_f32 = 2 * int(emit_f32)
    ys = []
    for h in range(acc.shape[1] // head_dim):
        y = acc[:, h * head_dim:(h + 1) * head_dim]
        if head_norm:
            y = _rms(y, hg_ref[...])
        if scale != 1.0:
            y = y * scale
        if emit_t:
            outs[n_f32 + 1][h] = y.T.astype(BF16)
        ys.append(y)
    y = jnp.concatenate(ys, axis=1)
    outs[n_f32][...] = y.astype(BF16)
    if emit_f32:
        _pair_store(outs[0], outs[1], n_p, y)


def _head_proj(a, g, w, head_gain, *, col0, n_p, head_dim, head_norm, scale, emit_f32, emit_t, tn, name):
    m, d = a.shape
    n = d
    jb = col0 // tn
    out_spec = pl.BlockSpec((ROW_TILE, tn), lambda j, i: (i, j))
    specs, shapes = [out_spec], [jax.ShapeDtypeStruct((m, n), BF16)]
    if emit_f32:
        specs = _pair_specs(n_p, tn, lambda j: j) + specs
        shapes = [jax.ShapeDtypeStruct((n_p * ROW_TILE, n), F32),
                  jax.ShapeDtypeStruct((m - n_p * ROW_TILE, n), F32)] + shapes
    if emit_t:
        specs.append(pl.BlockSpec((tn // head_dim, head_dim, ROW_TILE), lambda j, i: (j, 0, i)))
        shapes.append(jax.ShapeDtypeStruct((n // head_dim, head_dim, m), BF16))
    return pl.pallas_call(
        functools.partial(_head_proj_kernel, n_p=n_p, head_dim=head_dim, head_norm=head_norm,
                          scale=scale, emit_f32=emit_f32, emit_t=emit_t),
        grid=(n // tn, m // ROW_TILE),
        in_specs=[pl.BlockSpec((ROW_TILE, d), lambda j, i: (i, 0)),
                  pl.BlockSpec((d, 1), lambda j, i: (0, 0)),
                  pl.BlockSpec((d, tn), lambda j, i: (0, jb + j)),
                  pl.BlockSpec((1, head_dim), lambda j, i: (0, 0))],
        out_specs=specs,
        out_shape=shapes,
        scratch_shapes=[pltpu.VMEM((d, tn), BF16)],
        compiler_params=_params("parallel", "arbitrary"),
        name=name,
    )(a, g.reshape(d, 1), w, head_gain.reshape(1, head_dim))


def _ffn_in_kernel(a_ref, g_ref, wg_ref, wu_ref, o_ref, wg_scr, wu_scr):
    _cast_weight(wg_scr, wg_ref, g_ref)
    _cast_weight(wu_scr, wu_ref, g_ref)
    a = a_ref[...]
    af = a.astype(F32)
    r = _inv_rms(af)
    gate = r * _dot(a, wg_scr[...])
    up = r * _dot(a, wu_scr[...])
    o_ref[...] = (_silu(gate) * up).astype(o_ref.dtype)


def _ffn_in(a, g, w_in, *, layer, tn, name):
    m, d = a.shape
    ff = w_in.shape[2] // 2
    nj = ff // tn
    return pl.pallas_call(
        _ffn_in_kernel,
        grid=(nj, m // ROW_TILE),
        in_specs=[pl.BlockSpec((ROW_TILE, d), lambda j, i: (i, 0)),
                  pl.BlockSpec((d, 1), lambda j, i: (0, 0)),
                  pl.BlockSpec((None, d, tn), lambda j, i: (layer, 0, j)),
                  pl.BlockSpec((None, d, tn), lambda j, i: (layer, 0, j + nj))],
        out_specs=pl.BlockSpec((ROW_TILE, tn), lambda j, i: (i, j)),
        out_shape=jax.ShapeDtypeStruct((m, ff), BF16),
        scratch_shapes=[pltpu.VMEM((d, tn), BF16), pltpu.VMEM((d, tn), BF16)],
        compiler_params=_params("parallel", "arbitrary"),
        name=name,
    )(a, g.reshape(d, 1), w_in, w_in)


def _res_kernel(a_ref, w_ref, *rest, n_p, res_pair, out_pair):
    w_scr = rest[-1]
    n_res = 2 if res_pair else 1
    res_refs, out_refs = rest[:n_res], rest[n_res:-1]
    _cast_weight(w_scr, w_ref)
    res = _pair_load(res_refs[0], res_refs[1], n_p) if res_pair else res_refs[0][...]
    y = res + _dot(a_ref[...], w_scr[...])
    if out_pair:
        _pair_store(out_refs[0], out_refs[1], n_p, y)
    else:
        out_refs[0][...] = y
        out_refs[1][...] = y.astype(BF16)


def _res_matmul(a, w, res, *, layer, n_p, tn, out_pair, name, vmem=VMEM_LIMIT):
    m, k = a.shape
    n = w.shape[2]
    res_pair = isinstance(res, tuple)
    tile_spec = pl.BlockSpec((ROW_TILE, tn), lambda j, i: (i, j))
    res_specs = _pair_specs(n_p, tn, lambda j: j) if res_pair else [tile_spec]
    if out_pair:
        out_specs = _pair_specs(n_p, tn, lambda j: j)
        out_shape = [jax.ShapeDtypeStruct((n_p * ROW_TILE, n), F32),
                     jax.ShapeDtypeStruct((m - n_p * ROW_TILE, n), F32)]
    else:
        out_specs = [tile_spec, tile_spec]
        out_shape = [jax.ShapeDtypeStruct((m, n), F32), jax.ShapeDtypeStruct((m, n), BF16)]
    return pl.pallas_call(
        functools.partial(_res_kernel, n_p=n_p, res_pair=res_pair, out_pair=out_pair),
        grid=(n // tn, m // ROW_TILE),
        in_specs=[pl.BlockSpec((ROW_TILE, k), lambda j, i: (i, 0)),
                  pl.BlockSpec((None, k, tn), lambda j, i: (layer, 0, j))] + res_specs,
        out_specs=out_specs,
        out_shape=out_shape,
        scratch_shapes=[pltpu.VMEM((k, tn), BF16)],
        compiler_params=_params("parallel", "arbitrary", vmem=vmem),
        name=name,
    )(a, w, *(res if res_pair else (res,)))


def _gla_gate_kernel(xp_ref, xs_ref, g_ref, wl_ref, w2_ref, b_ref, o_ref, *, n_p):
    x = jnp.where(pl.program_id(0) < n_p, xp_ref[...], xs_ref[...])
    u = _rms(x, g_ref[...]).astype(BF16)
    low = _dot(u, wl_ref[...]).astype(BF16)
    o_ref[...] = _log_sigmoid(_dot(low, w2_ref[...]) + b_ref[...]) * (1.0 / GLA_TAU)


def _gla_gate(x_pair, g, w_low, w_gk2, b_gk, *, name):
    xp, xs = x_pair
    d = xp.shape[1]
    n_p = xp.shape[0] // ROW_TILE
    m = xp.shape[0] + xs.shape[0]
    n = w_gk2.shape[1]
    return pl.pallas_call(
        functools.partial(_gla_gate_kernel, n_p=n_p),
        grid=(n_p + 1,),
        in_specs=[pl.BlockSpec((ROW_TILE, d), lambda i: (jnp.minimum(i, n_p - 1), 0)),
                  pl.BlockSpec((ROW_TILE, d), lambda i: (0, 0)),
                  pl.BlockSpec((1, d), lambda i: (0, 0)),
                  pl.BlockSpec((d, LANES), lambda i: (0, 0)),
                  pl.BlockSpec((LANES, n), lambda i: (0, 0)),
                  pl.BlockSpec((1, n), lambda i: (0, 0))],
        out_specs=pl.BlockSpec((ROW_TILE, n), lambda i: (i, 0)),
        out_shape=jax.ShapeDtypeStruct((m, n), F32),
        compiler_params=_params("arbitrary"),
        name=name,
    )(xp, xs, g.reshape(1, d), w_low, w_gk2, b_gk.reshape(1, n))


def _logf_kernel(x_ref, g_ref, w_ref, wt_ref, b_ref, bt_ref, o_ref, ot_ref):
    u = _rms(x_ref[...], g_ref[...]).astype(BF16)
    o_ref[...] = _log_sigmoid(_dot(u, w_ref[...]) + b_ref[...])
    ot_ref[...] = _log_sigmoid(_dot_nt(wt_ref[...], u) + bt_ref[...])


def _logf_proj(x, g, w_f, b_f, *, name):
    m, d = x.shape
    return pl.pallas_call(
        _logf_kernel,
        grid=(m // ROW_TILE,),
        in_specs=[pl.BlockSpec((ROW_TILE, d), lambda i: (i, 0)),
                  pl.BlockSpec((1, d), lambda i: (0, 0)),
                  pl.BlockSpec((d, LANES), lambda i: (0, 0)),
                  pl.BlockSpec((LANES, d), lambda i: (0, 0)),
                  pl.BlockSpec((1, LANES), lambda i: (0, 0)),
                  pl.BlockSpec((LANES, 1), lambda i: (0, 0))],
        out_specs=[pl.BlockSpec((ROW_TILE, LANES), lambda i: (i, 0)),
                   pl.BlockSpec((LANES, ROW_TILE), lambda i: (0, i))],
        out_shape=[jax.ShapeDtypeStruct((m, LANES), F32),
                   jax.ShapeDtypeStruct((LANES, m), F32)],
        compiler_params=_params("parallel"),
        name=name,
    )(x, g.reshape(1, d), w_f, w_f.T, b_f.reshape(1, LANES), b_f.reshape(LANES, 1))


def _block_ref_rows(b, h):
    c, d = b.shape
    span = 2 * h
    if span >= 8:
        b3 = b.reshape(c // span, span, d)
        return jnp.broadcast_to(b3[:, h:h + 1, :], b3.shape).reshape(c, d)
    b3 = b.reshape(c // 8, 8, d)
    sub = lax.broadcasted_iota(jnp.int32, b3.shape, 1)
    ref = jnp.broadcast_to(b3[:, h:h + 1, :], b3.shape)
    for blk in range(1, 8 // span):
        cand = jnp.broadcast_to(b3[:, blk * span + h:blk * span + h + 1, :], b3.shape)
        ref = jnp.where(sub >= blk * span, cand, ref)
    return ref.reshape(c, d)


def _gla_kernel(q_ref, k_ref, v_ref, gate_ref, gk_ref, s0_ref, gout_ref, *rest, chunk):
    o_ref, sn_ref, s_scr = rest[-3:]
    c_idx = pl.program_id(2)

    @pl.when(c_idx == 0)
    def _():
        s_scr[...] = s0_ref[0, 0]

    dk = q_ref.shape[1]
    q = q_ref[...].astype(F32) * (dk ** -0.5)
    k = k_ref[...].astype(F32)
    v = v_ref[...]
    row = lax.broadcasted_iota(jnp.int32, (chunk, chunk), 0)
    col = lax.broadcasted_iota(jnp.int32, (chunk, chunk), 1)
    tri = jnp.where(row >= col, 1.0, 0.0).astype(BF16)
    g3 = _split3(gk_ref[...])
    b = _sum3(_dot, g3, tri, other_first=True)
    b_end = b[chunk - 1:chunk, :]
    b_end_col = _sum3(_dot_tn, g3, jnp.ones((chunk, LANES), BF16), other_first=False)

    s_prev = s_scr[...]
    o = _dot((q * jnp.exp(b)).astype(BF16), s_prev.astype(BF16))

    att = jnp.where(row == col, _dot_nt(q.astype(BF16), k.astype(BF16)), 0.0)
    split_bit = jnp.where(row > col, row ^ col, 0)
    sub_row = lax.broadcasted_iota(jnp.int32, (chunk, dk), 0)
    h = 1
    while h < chunk:
        f = jnp.exp(-jnp.abs(b - _block_ref_rows(b, h)))
        z = (jnp.where((sub_row & h) != 0, q, k) * f).astype(BF16)
        att = jnp.where(split_bit >= h, _dot_nt(z, z), att)
        h *= 2
    o = o + _dot(att.astype(BF16), v)

    k_dec = (k * jnp.exp(b_end - b)).astype(BF16)
    decay = jnp.exp(b_end_col)
    decay = jnp.concatenate([decay] * (s_prev.shape[1] // LANES), axis=1)
    s_new = decay * s_prev + _dot_tn(k_dec, v)
    s_scr[...] = s_new

    @pl.when(c_idx == pl.num_programs(2) - 1)
    def _():
        sn_ref[0, 0] = s_new

    o_ref[...] = (_rms(o, gout_ref[...]) * _silu(gate_ref[...].astype(F32))).astype(o_ref.dtype)


def _gla_scan(proj, gk, s0, g_out, *, row0, streams, length, chunk, out_buf, name):
    heads = GLA_HEADS
    dk = gk.shape[1] // heads
    dv = s0.shape[-1]
    n_chunks = length // chunk
    base = row0 // chunk
    kq, vq = dk * heads // dk, 2 * dk * heads // dv

    def rows(s, h, c):
        return base + s * n_chunks + c

    in_specs = [pl.BlockSpec((chunk, dk), lambda s, h, c: (rows(s, h, c), h)),
                pl.BlockSpec((chunk, dk), lambda s, h, c: (rows(s, h, c), kq + h)),
                pl.BlockSpec((chunk, dv), lambda s, h, c: (rows(s, h, c), vq + h)),
                pl.BlockSpec((chunk, dv), lambda s, h, c: (rows(s, h, c), vq + heads + h)),
                pl.BlockSpec((chunk, dk), lambda s, h, c: (rows(s, h, c), h)),
                pl.BlockSpec((1, 1, dk, dv), lambda s, h, c: (s, h, 0, 0)),
                pl.BlockSpec((1, dv), lambda s, h, c: (0, 0))]
    args = [proj, proj, proj, proj, gk, s0, g_out.reshape(1, dv)]
    aliases = {}
    if out_buf is not None:
        in_specs.append(pl.BlockSpec(memory_space=pl.ANY))
        args.append(out_buf)
        aliases = {len(args) - 1: 0}
    return pl.pallas_call(
        functools.partial(_gla_kernel, chunk=chunk),
        grid=(streams, heads, n_chunks),
        in_specs=in_specs,
        out_specs=[pl.BlockSpec((chunk, dv), lambda s, h, c: (rows(s, h, c), h)),
                   pl.BlockSpec((1, 1, dk, dv), lambda s, h, c: (s, h, 0, 0))],
        out_shape=[jax.ShapeDtypeStruct((proj.shape[0], heads * dv), BF16),
                   jax.ShapeDtypeStruct((streams, heads, dk, dv), F32)],
        scratch_shapes=[pltpu.VMEM((dk, dv), F32)],
        input_output_aliases=aliases,
        compiler_params=_params("parallel", "parallel", "arbitrary"),
        name=name,
    )(*args)


def _bias_lane_maps(heads):
    maps = np.zeros((heads, 3 * LANES, 2 * LANES), np.float32)
    for h in range(heads):
        for piece in range(3):
            maps[h, piece * LANES + h, 3 + piece] = -1.0
            maps[h, piece * LANES + h, LANES + piece] = 1.0
    const = np.zeros((1, 2 * LANES), np.float32)
    const[0, 0:3] = 1.0
    const[0, LANES + 3:LANES + 6] = 1.0
    return jnp.asarray(maps, BF16), jnp.asarray(const)


def _cumsum_bias_kernel(x_ref, maps_ref, const_ref, ka_ref, qa_ref, carry_scr, *, tile, heads):
    @pl.when(pl.program_id(0) == 0)
    def _():
        carry_scr[...] = jnp.zeros_like(carry_scr)

    r = lax.broadcasted_iota(jnp.int32, (tile, tile), 0)
    c = lax.broadcasted_iota(jnp.int32, (tile, tile), 1)
    x = x_ref[...]
    lower = jnp.where(r >= c, 1.0, 0.0).astype(BF16)
    cum = carry_scr[...] + _sum3(_dot, _split3(x), lower, other_first=True)
    carry_scr[...] += jnp.sum(x, axis=0, keepdims=True)
    pieces = jnp.concatenate(_split3(cum * LOG2E), axis=1)
    for h in range(heads):
        lanes = (_dot(pieces, maps_ref[h]) + const_ref[...]).astype(BF16)
        ka_ref[h] = lanes[:, :LANES]
        qa_ref[h] = lanes[:, LANES:]


def _cumsum_bias(x, *, length, heads, tile, name):
    maps, const = _bias_lane_maps(heads)
    out = jax.ShapeDtypeStruct((heads, length, LANES), BF16)
    return pl.pallas_call(
        functools.partial(_cumsum_bias_kernel, tile=tile, heads=heads),
        grid=(length // tile,),
        in_specs=[pl.BlockSpec((tile, LANES), lambda i: (i, 0)),
                  pl.BlockSpec(maps.shape, lambda i: (0, 0, 0)),
                  pl.BlockSpec(const.shape, lambda i: (0, 0))],
        out_specs=[pl.BlockSpec((heads, tile, LANES), lambda i: (0, i, 0)),
                   pl.BlockSpec((heads, tile, LANES), lambda i: (0, i, 0))],
        out_shape=[out, out],
        scratch_shapes=[pltpu.VMEM((1, LANES), F32)],
        compiler_params=_params("arbitrary"),
        name=name,
    )(x, maps, const)


def _cumsum_cached_kernel(past_ref, pastt_ref, new_ref, newt_ref, cq_ref, cpt_ref, cnt_ref, *, tile):
    past_t = pastt_ref[0]
    past_len = past_t.shape[1]
    r = lax.broadcasted_iota(jnp.int32, (tile, tile), 0)
    c = lax.broadcasted_iota(jnp.int32, (tile, tile), 1)
    upper = jnp.where(r <= c, 1.0, 0.0).astype(BF16)
    carry = jnp.zeros((past_t.shape[0], 1), F32)
    for t in range(past_len // tile):
        xt = past_t[:, t * tile:(t + 1) * tile]
        cpt_ref[0, :, t * tile:(t + 1) * tile] = carry + _sum3(_dot, _split3(xt), upper, other_first=False)
        carry = carry + jnp.sum(xt, axis=1, keepdims=True)
    n = newt_ref.shape[2]
    rn = lax.broadcasted_iota(jnp.int32, (n, n), 0)
    cn = lax.broadcasted_iota(jnp.int32, (n, n), 1)
    cnt_ref[0] = carry + _sum3(_dot, _split3(newt_ref[0]),
                               jnp.where(rn <= cn, 1.0, 0.0).astype(BF16), other_first=False)
    heads = past_t.shape[0]
    past_total = jnp.sum(past_ref[0], axis=0, keepdims=True)
    c_new = _sum3(_dot, _split3(new_ref[...]), jnp.where(rn >= cn, 1.0, 0.0).astype(BF16), other_first=True)
    cq_ref[0] = past_total + c_new[:, :heads]


def _cumsum_cached(past, past_t, new, new_t, *, row0, name):
    streams, past_len, heads = past.shape
    n = new_t.shape[2]
    base = row0 // n
    return pl.pallas_call(
        functools.partial(_cumsum_cached_kernel, tile=512),
        grid=(streams,),
        in_specs=[pl.BlockSpec((1, past_len, heads), lambda s: (s, 0, 0)),
                  pl.BlockSpec((1, heads, past_len), lambda s: (s, 0, 0)),
                  pl.BlockSpec((n, LANES), lambda s: (base + s, 0)),
                  pl.BlockSpec((1, heads, n), lambda s: (s, 0, 0))],
        out_specs=[pl.BlockSpec((1, n, heads), lambda s: (s, 0, 0)),
                   pl.BlockSpec((1, heads, past_len), lambda s: (s, 0, 0)),
                   pl.BlockSpec((1, heads, n), lambda s: (s, 0, 0))],
        out_shape=[jax.ShapeDtypeStruct((streams, n, heads), F32),
                   jax.ShapeDtypeStruct((streams, heads, past_len), F32),
                   jax.ShapeDtypeStruct((streams, heads, n), F32)],
        compiler_params=_params("parallel"),
        name=name,
    )(past, past_t, new, new_t)


def _fox_prompt_kernel(qi_ref, kj_ref, q_ref, qa_ref, k_ref, ka_ref, vt_ref, buf_ref, o_ref,
                       m_scr, acc_scr, p_scr, *, hp):
    step = pl.program_id(1)
    qi = qi_ref[step]
    kj = kj_ref[step]
    tq, tk = q_ref.shape[0], k_ref.shape[0]
    hd = q_ref.shape[1] // hp

    @pl.when(kj == 0)
    def _():
        m_scr[...] = jnp.full_like(m_scr, NEG_BIG)
        acc_scr[...] = jnp.zeros_like(acc_scr)

    units = [(h, c) for h in range(hp) for c in range(tq // QUERY_CHUNK)]

    def scores(unit, masked):
        h, c = unit
        sl = slice(h * hd, (h + 1) * hd)
        rows = slice(c * QUERY_CHUNK, (c + 1) * QUERY_CHUNK)
        keys = (c + 1) * QUERY_CHUNK if masked else tk
        q_aug = jnp.concatenate([q_ref[rows, sl], qa_ref[h, rows, :]], axis=1)
        k_aug = jnp.concatenate([k_ref[:keys, sl], ka_ref[h, :keys, :]], axis=1)
        s = _dot_nt(k_aug, q_aug)
        if masked:
            key = lax.broadcasted_iota(jnp.int32, s.shape, 0)
            qry = lax.broadcasted_iota(jnp.int32, s.shape, 1) + c * QUERY_CHUNK
            s = jnp.where(key <= qry, s, NEG_BIG)
        return s

    def update(masked):
        s_next = scores(units[0], masked)
        for i, (h, c) in enumerate(units):
            s = s_next
            if i + 1 < len(units):
                s_next = scores(units[i + 1], masked)
            keys = s.shape[0]
            cols = slice(c * QUERY_CHUNK, (c + 1) * QUERY_CHUNK)
            m_prev = m_scr[h, :, cols]
            m_new = jnp.maximum(m_prev, jnp.max(s, axis=0, keepdims=True))
            alpha = jnp.exp2(m_prev - m_new)
            m_rows = jnp.broadcast_to(m_new, (8, QUERY_CHUNK))
            for r0 in range(0, keys, EXP_ROWS):
                sc = s[r0:r0 + EXP_ROWS, :].reshape(EXP_ROWS // 8, 8, QUERY_CHUNK)
                p_scr[i % 2, r0:r0 + EXP_ROWS, :] = (
                    jnp.exp2(sc - m_rows).reshape(EXP_ROWS, QUERY_CHUNK).astype(BF16))
            vt_aug = jnp.concatenate([vt_ref[h, :, :keys], jnp.ones((SUM_ROWS, keys), BF16)], axis=0)
            acc_scr[h, :, cols] = alpha * acc_scr[h, :, cols] + _dot(vt_aug, p_scr[i % 2, :keys, :])
            m_scr[h, :, cols] = m_new

    @pl.when(kj < qi)
    def _():
        update(False)

    @pl.when(kj == qi)
    def _():
        update(True)
        for h in range(hp):
            acc = acc_scr[h]
            o_ref[:, h * hd:(h + 1) * hd] = (acc[:hd] / acc[hd:hd + 1]).T.astype(o_ref.dtype)


def _fox_prompt(q, k, v_t, ka, qa, out_buf, *, length, tile, hp, name):
    heads, hd, _ = v_t.shape
    nq = length // tile
    pairs = [(i, j) for i in range(nq) for j in range(i + 1)]
    qi = jnp.asarray(np.array([p[0] for p in pairs], np.int32))
    kj = jnp.asarray(np.array([p[1] for p in pairs], np.int32))
    grid_spec = pltpu.PrefetchScalarGridSpec(
        num_scalar_prefetch=2,
        grid=(heads // hp, len(pairs)),
        in_specs=[pl.BlockSpec((tile, hp * hd), lambda g, s, qi, kj: (qi[s], g)),
                  pl.BlockSpec((hp, tile, LANES), lambda g, s, qi, kj: (g, qi[s], 0)),
                  pl.BlockSpec((tile, hp * hd), lambda g, s, qi, kj: (kj[s], g)),
                  pl.BlockSpec((hp, tile, LANES), lambda g, s, qi, kj: (g, kj[s], 0)),
                  pl.BlockSpec((hp, hd, tile), lambda g, s, qi, kj: (g, 0, kj[s])),
                  pl.BlockSpec(memory_space=pl.ANY)],
        out_specs=pl.BlockSpec((tile, hp * hd), lambda g, s, qi, kj: (qi[s], g)),
        scratch_shapes=[pltpu.VMEM((hp, 1, tile), F32),
                        pltpu.VMEM((hp, hd + SUM_ROWS, tile), F32),
                        pltpu.VMEM((2, tile, QUERY_CHUNK), BF16)],
    )
    return pl.pallas_call(
        functools.partial(_fox_prompt_kernel, hp=hp),
        grid_spec=grid_spec,
        out_shape=jax.ShapeDtypeStruct(out_buf.shape, BF16),
        input_output_aliases={7: 0},
        compiler_params=_params("parallel", "arbitrary"),
        name=name,
    )(qi, kj, q, qa, k, ka, v_t, out_buf)


def _fox_cached_kernel(q_ref, kn_ref, vn_ref, kp_ref, vp_ref, cq_ref, cp_ref, cn_ref, buf_ref, o_ref,
                       m_scr, l_scr, acc_scr, *, heads, tp):
    t = pl.program_id(1)
    n = q_ref.shape[0]
    hd = q_ref.shape[1] // heads

    @pl.when(t == 0)
    def _():
        m_scr[...] = jnp.full_like(m_scr, NEG_BIG)
        l_scr[...] = jnp.zeros_like(l_scr)
        acc_scr[...] = jnp.zeros_like(acc_scr)

    def update(h, s, v):
        m_prev = m_scr[h]
        m_new = jnp.maximum(m_prev, jnp.max(s, axis=1, keepdims=True))
        alpha = jnp.exp2(m_prev - m_new)
        p = jnp.exp2(s - m_new)
        l_scr[h] = alpha * l_scr[h] + jnp.sum(p, axis=1, keepdims=True)
        acc_scr[h] = alpha * acc_scr[h] + _dot(p.astype(BF16), v)
        m_scr[h] = m_new

    for h in range(heads):
        sl = slice(h * hd, (h + 1) * hd)
        k = kp_ref[0, pl.ds(h, tp, stride=heads), :].astype(BF16)
        v = vp_ref[0, pl.ds(h, tp, stride=heads), :].astype(BF16)
        bias = (cq_ref[0][:, h:h + 1] - cp_ref[0, h]) * LOG2E
        update(h, _dot_nt(q_ref[:, sl], k) + bias, v)

    @pl.when(t == pl.num_programs(1) - 1)
    def _():
        row = lax.broadcasted_iota(jnp.int32, (n, n), 0)
        col = lax.broadcasted_iota(jnp.int32, (n, n), 1)
        for h in range(heads):
            sl = slice(h * hd, (h + 1) * hd)
            bias = (cq_ref[0][:, h:h + 1] - cn_ref[0, h]) * LOG2E
            s = jnp.where(col <= row, _dot_nt(q_ref[:, sl], kn_ref[:, sl]) + bias, NEG_BIG)
            update(h, s, vn_ref[:, sl])
            o_ref[:, sl] = (acc_scr[h] / l_scr[h]).astype(o_ref.dtype)


def _fox_cached(q, k_new, v_new, cache_k, cache_v, cq, cp_t, cn_t, out_buf, *, row0, tp, name):
    streams, n, heads = cq.shape
    hd = cache_k.shape[2]
    past_len = cache_k.shape[1] // heads
    base = row0 // n
    return pl.pallas_call(
        functools.partial(_fox_cached_kernel, heads=heads, tp=tp),
        grid=(streams, past_len // tp),
        in_specs=[pl.BlockSpec((n, heads * hd), lambda s, t: (base + s, 0)),
                  pl.BlockSpec((n, heads * hd), lambda s, t: (base + s, 0)),
                  pl.BlockSpec((n, heads * hd), lambda s, t: (base + s, 0)),
                  pl.BlockSpec((1, tp * heads, hd), lambda s, t: (s, t, 0)),
                  pl.BlockSpec((1, tp * heads, hd), lambda s, t: (s, t, 0)),
                  pl.BlockSpec((1, n, heads), lambda s, t: (s, 0, 0)),
                  pl.BlockSpec((1, heads, 1, tp), lambda s, t: (s, 0, 0, t)),
                  pl.BlockSpec((1, heads, 1, n), lambda s, t: (s, 0, 0, 0)),
                  pl.BlockSpec(memory_space=pl.ANY)],
        out_specs=pl.BlockSpec((n, heads * hd), lambda s, t: (base + s, 0)),
        out_shape=jax.ShapeDtypeStruct(out_buf.shape, BF16),
        scratch_shapes=[pltpu.VMEM((heads, n, 1), F32), pltpu.VMEM((heads, n, 1), F32),
                        pltpu.VMEM((heads, n, hd), F32)],
        input_output_aliases={8: 0},
        compiler_params=_params("parallel", "arbitrary"),
        name=name,
    )(q, k_new, v_new, cache_k, cache_v, cq,
      cp_t.reshape(streams, heads, 1, past_len), cn_t.reshape(streams, heads, 1, n), out_buf)


def kernel(x_prompt, x_sample, state_gla, cache_k, cache_v, cache_logf, g_mix, g_ffn, gla_w_in, gla_w_gk2, gla_b_gk, gla_g_out, gla_w_out, kv_g, kv_w, kv_b_f, kv_g_k, fox_w_q, fox_g_q, fox_w_o, ffn_w_in, ffn_w_out):
    batch, seq, d = x_prompt.shape
    dec_batch, dec_seq, _ = x_sample.shape
    heads = FOX_HEADS
    hd = d // heads
    n_p = batch * seq
    n_s = dec_batch * dec_seq
    assert batch == 1 and g_mix.shape[0] == 2 and state_gla.shape[0] == 1
    assert n_p % ROW_TILE == 0 and n_s == ROW_TILE
    p_tiles = n_p // ROW_TILE
    gla_hk = gla_w_gk2.shape[2]
    gla_hv = gla_w_out.shape[1]
    gla_main = 2 * gla_hk + 2 * gla_hv
    gla_dk, gla_dv = gla_hk // GLA_HEADS, gla_hv // GLA_HEADS
    past_len = cache_k.shape[1]

    x_pair = (x_prompt.reshape(n_p, d), x_sample.reshape(n_s, d))

    w_gla = gla_w_in[0]
    proj = _x_proj(x_pair, g_mix[0], w_gla, n_cols=gla_main, tn=1024, name="gla_proj")
    w_low = jnp.pad(w_gla[:, gla_main:], ((0, 0), (0, LANES - GLA_LOWRANK))).astype(BF16)
    w_gk2 = jnp.pad(gla_w_gk2[0], ((0, LANES - GLA_LOWRANK), (0, 0))).astype(BF16)
    gk = _gla_gate(x_pair, g_mix[0], w_low, w_gk2, gla_b_gk[0], name="gla_gate")
    s0_prompt = jnp.zeros((batch, GLA_HEADS, gla_dk, gla_dv), F32)
    og = jnp.zeros((n_p + n_s, gla_hv), BF16)
    og, state_p = _gla_scan(proj, gk, s0_prompt, gla_g_out[0], row0=0, streams=batch, length=seq,
                            chunk=256, out_buf=og, name="gla_scan_prompt")
    og, state_s = _gla_scan(proj, gk, state_gla[0], gla_g_out[0], row0=n_p, streams=dec_batch,
                            length=dec_seq, chunk=dec_seq, out_buf=og, name="gla_scan_sample")
    h, h16 = _res_matmul(og, gla_w_out, x_pair, layer=0, n_p=p_tiles, tn=1024, out_pair=False,
                         name="gla_out")

    act = _ffn_in(h16, g_ffn[0], ffn_w_in, layer=0, tn=512, name="ffn0_in")
    h, h16 = _res_matmul(act, ffn_w_out, h, layer=0, n_p=p_tiles, tn=512, out_pair=False, name="ffn0_out",
                         vmem=VMEM_LIMIT_WIDE_K)

    k_p, k_s, k16 = _head_proj(h16, kv_g, kv_w, kv_g_k, col0=0, n_p=p_tiles, head_dim=hd, head_norm=True,
                               scale=1.0, emit_f32=True, emit_t=False, tn=1024, name="kv_k")
    v_p, v_s, v16, v16_t = _head_proj(h16, kv_g, kv_w, kv_g_k, col0=d, n_p=p_tiles, head_dim=hd,
                                      head_norm=False, scale=1.0, emit_f32=True, emit_t=True, tn=1024,
                                      name="kv_v")
    (q16,) = _head_proj(h16, g_mix[1], fox_w_q[0], fox_g_q[0], col0=0, n_p=p_tiles, head_dim=hd,
                        head_norm=True, scale=hd ** -0.5 * LOG2E, emit_f32=False, emit_t=False, tn=1024,
                        name="fox_q")
    w_f = jnp.pad(kv_w[:, 2 * d:], ((0, 0), (0, LANES - heads))).astype(BF16)
    b_f = jnp.pad(kv_b_f, (0, LANES - heads))
    logf, logf_t = _logf_proj(h, kv_g, w_f, b_f, name="kv_logf")

    ka, qa = _cumsum_bias(logf, length=n_p, heads=heads, tile=512, name="logf_cumsum_prompt")
    attn = _fox_prompt(q16, k16, v16_t, ka, qa, jnp.zeros((n_p + n_s, d), BF16), length=n_p, tile=1024, hp=2,
                       name="fox_attn_prompt")
    new_t = logf_t[:heads, n_p:].reshape(heads, dec_batch, dec_seq).transpose(1, 0, 2)
    cq_s, cp_t, cn_t = _cumsum_cached(cache_logf, cache_logf.transpose(0, 2, 1), logf, new_t,
                                      row0=n_p, name="logf_cumsum_sample")
    attn = _fox_cached(q16, k16, v16, cache_k.reshape(dec_batch, past_len * heads, hd),
                       cache_v.reshape(dec_batch, past_len * heads, hd), cq_s, cp_t, cn_t, attn,
                       row0=n_p, tp=1024, name="fox_attn_sample")
    h, h16 = _res_matmul(attn, fox_w_o, h, layer=0, n_p=p_tiles, tn=1024, out_pair=False, name="fox_out")

    act = _ffn_in(h16, g_ffn[1], ffn_w_in, layer=1, tn=512, name="ffn1_in")
    y_p, y_s = _res_matmul(act, ffn_w_out, h, layer=1, n_p=p_tiles, tn=512, out_pair=True, name="ffn1_out",
                           vmem=VMEM_LIMIT_WIDE_K)

    lf = logf[:, :heads]
    return (y_p.reshape(batch, seq, d), y_s.reshape(dec_batch, dec_seq, d), state_p[None],
            k_p.reshape(batch, seq, heads, hd), v_p.reshape(batch, seq, heads, hd),
            lf[:n_p].reshape(batch, seq, heads), state_s[None],
            k_s.reshape(dec_batch, dec_seq, heads, hd), v_s.reshape(dec_batch, dec_seq, heads, hd),
            lf[n_p:].reshape(dec_batch, dec_seq, heads))
```
